```python
import math
import jax, jax.numpy as jnp
from jax import lax
import numpy as np

D_MODEL = 2048
BATCH = 8
SEQ = 4096
DEPTH = 4
DEC_BATCH = 2
DEC_SEQ = 8192
PAST_LEN = 128

GRID_W = 64
Q_BLOCK = 128
HEAD_DIM = 128
ROPE_THETA = 10000.0
EPS = 1e-6
A_HEADS = D_MODEL // HEAD_DIM
A_KV_HEADS = A_HEADS // 4
A_GROUP = A_HEADS // A_KV_HEADS
A_WIDTH = A_HEADS * HEAD_DIM
A_KV_WIDTH = A_KV_HEADS * HEAD_DIM
A_IN = 2 * A_WIDTH + 2 * A_KV_WIDTH
B_HEADS = D_MODEL // (2 * HEAD_DIM)
B_WIDTH = B_HEADS * 2 * HEAD_DIM
B_IN = 4 * B_WIDTH
N_A = (DEPTH + 1) // 2
N_B = DEPTH // 2

kernel_name = "hybrid_gqa_axial_diffattn_encoder"


def rmsnorm(x, g):
    xf = x.astype(jnp.float32)
    y = xf * lax.rsqrt(jnp.mean(xf * xf, axis=-1, keepdims=True) + EPS)
    return (y * g.astype(jnp.float32)).astype(x.dtype)


def rope(x, pos, dim):
    inv = 1.0 / (ROPE_THETA ** (jnp.arange(0, dim, 2, dtype=jnp.float32) / dim))
    ang = pos[:, None] * inv[None, :]
    cos = jnp.cos(ang)[None, :, None, :]
    sin = jnp.sin(ang)[None, :, None, :]
    x1, x2 = jnp.split(x.astype(jnp.float32), 2, axis=-1)
    return jnp.concatenate([x1 * cos - x2 * sin, x2 * cos + x1 * sin], axis=-1).astype(x.dtype)


def axial_rope(x, row, col):
    half = HEAD_DIM // 2
    return jnp.concatenate([rope(x[..., :half], row, half), rope(x[..., half:], col, half)], axis=-1)


def gqa_axial_layer(x, norm_g, w_in, q_g, k_g, w_out):
    B, S, _ = x.shape
    h = rmsnorm(x, norm_g)
    proj = h @ w_in
    q, k, v, z = jnp.split(proj, [A_WIDTH, A_WIDTH + A_KV_WIDTH, A_WIDTH + 2 * A_KV_WIDTH], axis=-1)
    q = rmsnorm(q.reshape(B, S, A_HEADS, HEAD_DIM), q_g)
    k = rmsnorm(k.reshape(B, S, A_KV_HEADS, HEAD_DIM), k_g)
    v = v.reshape(B, S, A_KV_HEADS, HEAD_DIM)
    n_rows = S // GRID_W
    row = jnp.repeat(jnp.arange(n_rows, dtype=jnp.float32), GRID_W)
    col = jnp.tile(jnp.arange(GRID_W, dtype=jnp.float32), n_rows)
    q = axial_rope(q, row, col) * (HEAD_DIM ** -0.5)
    k = axial_rope(k, row, col)
    nb = S // Q_BLOCK
    qb = q.reshape(B, nb, Q_BLOCK, A_KV_HEADS, A_GROUP, HEAD_DIM).transpose(1, 0, 2, 3, 4, 5)

    def block(q_blk):
        s = jnp.einsum('bqhgd,bkhd->bhgqk', q_blk, k).astype(jnp.float32)
        p = jax.nn.softmax(s, axis=-1).astype(v.dtype)
        return jnp.einsum('bhgqk,bkhd->bqhgd', p, v)

    o = lax.map(block, qb).transpose(1, 0, 2, 3, 4, 5).reshape(B, S, A_WIDTH)
    return x + (o * jax.nn.silu(z)) @ w_out


def diff_attn_layer(x, layer_idx, norm_g, w_in, q_g, k_g, lq1, lk1, lq2, lk2, subln_g, w_out):
    B, S, _ = x.shape
    lam_init = 0.8 - 0.6 * math.exp(-0.3 * layer_idx)
    h = rmsnorm(x, norm_g)
    proj = h @ w_in
    q, k, v, z = jnp.split(proj, [B_WIDTH, 2 * B_WIDTH, 3 * B_WIDTH], axis=-1)
    q = rmsnorm(q.reshape(B, S, 2 * B_HEADS, HEAD_DIM), q_g)
    k = rmsnorm(k.reshape(B, S, 2 * B_HEADS, HEAD_DIM), k_g)
    v = v.reshape(B, S, B_HEADS, 2 * HEAD_DIM)
    pos = jnp.arange(S, dtype=jnp.float32)
    q = rope(q, pos, HEAD_DIM) * (HEAD_DIM ** -0.5)
    k = rope(k, pos, HEAD_DIM)
    f32 = jnp.float32
    lam = (jnp.exp(jnp.sum(lq1.astype(f32) * lk1.astype(f32)))
           - jnp.exp(jnp.sum(lq2.astype(f32) * lk2.astype(f32))) + lam_init)
    nb = S // Q_BLOCK
    qb = q.reshape(B, nb, Q_BLOCK, 2 * B_HEADS, HEAD_DIM).transpose(1, 0, 2, 3, 4)

    def block(q_blk):
        s = jnp.einsum('bqnd,bknd->bnqk', q_blk, k).astype(f32)
        p = jax.nn.softmax(s, axis=-1).reshape(B, B_HEADS, 2, Q_BLOCK, S)
        a = (p[:, :, 0] - lam * p[:, :, 1]).astype(v.dtype)
        return jnp.einsum('bhqk,bkhe->bqhe', a, v)

    o = lax.map(block, qb).transpose(1, 0, 2, 3, 4).reshape(B, S, B_HEADS, 2 * HEAD_DIM)
    o = (rmsnorm(o, subln_g) * (1.0 - lam_init)).reshape(B, S, B_WIDTH)
    return x + (o * jax.nn.silu(z)) @ w_out


def run_trunk(x, a_norm, a_w_in, a_q_norm, a_k_norm, a_w_out,
              b_norm, b_w_in, b_q_norm, b_k_norm, b_lambda_q1, b_lambda_k1,
              b_lambda_q2, b_lambda_k2, b_subln, b_w_out):
    for i in range(DEPTH):
        j = i // 2
        if i % 2 == 0:
            x = gqa_axial_layer(x, a_norm[j], a_w_in[j], a_q_norm[j], a_k_norm[j], a_w_out[j])
        else:
            x = diff_attn_layer(x, i, b_norm[j], b_w_in[j], b_q_norm[j], b_k_norm[j],
                                b_lambda_q1[j], b_lambda_k1[j], b_lambda_q2[j], b_lambda_k2[j],
                                b_subln[j], b_w_out[j])
    return x


def setup_inputs(seed: int = 0) -> dict:
    key = jax.random.key(seed)
    ks = jax.random.split(key, 17)
    f32 = jnp.float32
    nrm = lambda k, shape, s: jax.random.normal(k, shape, f32) * s
    gain = lambda k, shape: 1.0 + 0.02 * jax.random.normal(k, shape, f32)
    return {
        "x_prompt": jax.random.normal(ks[0], (BATCH, SEQ, D_MODEL), f32),
        "x_sample": jax.random.normal(ks[1], (DEC_BATCH, DEC_SEQ, D_MODEL), f32),
        "a_norm": gain(ks[2], (N_A, D_MODEL)),
        "a_w_in": nrm(ks[3], (N_A, D_MODEL, A_IN), D_MODEL ** -0.5),
        "a_q_norm": gain(ks[4], (N_A, HEAD_DIM)),
        "a_k_norm": gain(ks[5], (N_A, HEAD_DIM)),
        "a_w_out": nrm(ks[6], (N_A, A_WIDTH, D_MODEL), A_WIDTH ** -0.5),
        "b_norm": gain(ks[7], (N_B, D_MODEL)),
        "b_w_in": nrm(ks[8], (N_B, D_MODEL, B_IN), D_MODEL ** -0.5),
        "b_q_norm": gain(ks[9], (N_B, HEAD_DIM)),
        "b_k_norm": gain(ks[10], (N_B, HEAD_DIM)),
        "b_lambda_q1": nrm(ks[11], (N_B, HEAD_DIM), 0.1),
        "b_lambda_k1": nrm(ks[12], (N_B, HEAD_DIM), 0.1),
        "b_lambda_q2": nrm(ks[13], (N_B, HEAD_DIM), 0.1),
        "b_lambda_k2": nrm(ks[14], (N_B, HEAD_DIM), 0.1),
        "b_subln": gain(ks[15], (N_B, 2 * HEAD_DIM)),
        "b_w_out": nrm(ks[16], (N_B, B_WIDTH, D_MODEL), B_WIDTH ** -0.5),
    }


def reference(x_prompt, x_sample, a_norm, a_w_in, a_q_norm, a_k_norm, a_w_out,
              b_norm, b_w_in, b_q_norm, b_k_norm, b_lambda_q1, b_lambda_k1,
              b_lambda_q2, b_lambda_k2, b_subln, b_w_out):
    y_prompt = run_trunk(x_prompt, a_norm, a_w_in, a_q_norm, a_k_norm, a_w_out,
                         b_norm, b_w_in, b_q_norm, b_k_norm, b_lambda_q1, b_lambda_k1,
                         b_lambda_q2, b_lambda_k2, b_subln, b_w_out)
    y_sample = run_trunk(x_sample, a_norm, a_w_in, a_q_norm, a_k_norm, a_w_out,
                         b_norm, b_w_in, b_q_norm, b_k_norm, b_lambda_q1, b_lambda_k1,
                         b_lambda_q2, b_lambda_k2, b_subln, b_w_out)
    return (y_prompt, y_sample)
```

```python
import functools
import math

import jax
import jax.numpy as jnp
from jax import lax
from jax.experimental import pallas as pl
from jax.experimental.pallas import tpu as pltpu

D_MODEL = 2048
DEPTH = 4
GRID_W = 64
HEAD_DIM = 128
ROPE_THETA = 10000.0
EPS = 1e-6
A_HEADS = D_MODEL // HEAD_DIM
A_KV_HEADS = A_HEADS // 4
A_GROUP = A_HEADS // A_KV_HEADS
A_WIDTH = A_HEADS * HEAD_DIM
A_KV_WIDTH = A_KV_HEADS * HEAD_DIM
B_HEADS = D_MODEL // (2 * HEAD_DIM)
B_WIDTH = B_HEADS * 2 * HEAD_DIM

LANES = 128
VMEM_LIMIT_BYTES = 56 * 1024 * 1024

F32 = jnp.float32
BF16 = jnp.bfloat16


def _compiler_params(n_axes):
    return pltpu.CompilerParams(
        dimension_semantics=("arbitrary",) * n_axes,
        vmem_limit_bytes=VMEM_LIMIT_BYTES,
    )


def _inproj_kernel(x_ref, g_ref, w_ref, qtab_ref, ktab_ref,
                   q_out, k_out, v_out, z_out, h_ref, *, nq, nk, nv, shifts, tn):
    j = pl.program_id(1)

    @pl.when(j == 0)
    def _():
        x = x_ref[...]
        ms = jnp.mean(x * x, axis=-1, keepdims=True)
        h_ref[...] = (x * lax.rsqrt(ms + EPS) * g_ref[...]).astype(BF16)

    y = jnp.dot(h_ref[...], w_ref[...], preferred_element_type=F32)

    def norm_rope(tab_ref):
        outs = []
        for hh in range(tn // HEAD_DIM):
            yh = y[:, hh * HEAD_DIM:(hh + 1) * HEAD_DIM]
            r = lax.rsqrt(jnp.mean(yh * yh, axis=-1, keepdims=True) + EPS)
            acc = yh * tab_ref[0]
            for t, sh in enumerate(shifts):
                acc = acc + pltpu.roll(yh, sh, 1) * tab_ref[1 + t]
            outs.append(acc * r)
        return jnp.concatenate(outs, axis=1)

    @pl.when(j < nq)
    def _():
        q_out[...] = norm_rope(qtab_ref).astype(BF16)

    @pl.when((j >= nq) & (j < nq + nk))
    def _():
        k_out[...] = norm_rope(ktab_ref).astype(BF16)

    @pl.when((j >= nq + nk) & (j < nq + nk + nv))
    def _():
        v_out[...] = y.astype(BF16)

    @pl.when(j >= nq + nk + nv)
    def _():
        z_out[...] = y.astype(BF16)


def _inproj(x, norm_g, w, qtab, ktab, *, widths, shifts, seq, tm, tn):
    T, D = x.shape
    wq, wk, wv, wz = widths
    nq, nk, nv, nz = wq // tn, wk // tn, wv // tn, wz // tn
    n_tab = qtab.shape[0]
    pos_blocks = seq // tm
    grid = (T // tm, nq + nk + nv + nz)

    def clamp(j, lo, n):
        return jnp.clip(j - lo, 0, n - 1)

    kern = functools.partial(_inproj_kernel, nq=nq, nk=nk, nv=nv, shifts=shifts, tn=tn)
    return pl.pallas_call(
        kern,
        grid=grid,
        in_specs=[
            pl.BlockSpec((tm, D), lambda i, j: (i, 0)),
            pl.BlockSpec((1, D), lambda i, j: (0, 0)),
            pl.BlockSpec((D, tn), lambda i, j: (0, j)),
            pl.BlockSpec((n_tab, tm, LANES), lambda i, j: (0, i % pos_blocks, 0)),
            pl.BlockSpec((n_tab, tm, LANES), lambda i, j: (0, i % pos_blocks, 0)),
        ],
        out_specs=[
            pl.BlockSpec((tm, tn), lambda i, j: (i, clamp(j, 0, nq))),
            pl.BlockSpec((tm, tn), lambda i, j: (i, clamp(j, nq, nk))),
            pl.BlockSpec((tm, tn), lambda i, j: (i, clamp(j, nq + nk, nv))),
            pl.BlockSpec((tm, tn), lambda i, j: (i, clamp(j, nq + nk + nv, nz))),
        ],
        out_shape=[
            jax.ShapeDtypeStruct((T, wq), BF16),
            jax.ShapeDtypeStruct((T, wk), BF16),
            jax.ShapeDtypeStruct((T, wv), BF16),
            jax.ShapeDtypeStruct((T, wz), BF16),
        ],
        scratch_shapes=[pltpu.VMEM((tm, D), BF16)],
        compiler_params=_compiler_params(2),
        name="inproj",
    )(x, norm_g.reshape(1, D), w, qtab, ktab)


def _softmax_step(s, v, m_ref, l_ref, acc_ref, tk):
    dv = acc_ref.shape[1]
    m_prev = m_ref[...]
    m_next = jnp.maximum(m_prev, jnp.max(s, axis=1, keepdims=True))
    alpha = jnp.exp(m_prev - m_next)
    p = jnp.exp(s - jnp.tile(m_next, (1, tk // LANES)))
    l_ref[...] = alpha * l_ref[...] + jnp.sum(p, axis=1, keepdims=True)
    pv = jnp.dot(p.astype(BF16), v, preferred_element_type=F32)
    acc_ref[...] = acc_ref[...] * jnp.tile(alpha, (1, dv // LANES)) + pv
    m_ref[...] = m_next


def _attn_a_kernel(q_ref, k_ref, v_ref, o_ref, m_ref, l_ref, acc_ref, *, tk, group):
    tq = q_ref.shape[1]
    seq = k_ref.shape[1]
    q = q_ref[0]
    qs = jnp.concatenate([q[:, g * HEAD_DIM:(g + 1) * HEAD_DIM] for g in range(group)], axis=0)
    m_ref[...] = jnp.full(m_ref.shape, -jnp.inf, F32)
    l_ref[...] = jnp.zeros(l_ref.shape, F32)
    acc_ref[...] = jnp.zeros(acc_ref.shape, F32)

    def body(c, carry):
        start = pl.multiple_of(c * tk, tk)
        k = k_ref[0, pl.ds(start, tk), :]
        v = v_ref[0, pl.ds(start, tk), :]
        s = lax.dot_general(qs, k, (((1,), (1,)), ((), ())), preferred_element_type=F32)
        _softmax_step(s, v, m_ref, l_ref, acc_ref, tk)
        return carry

    lax.fori_loop(0, seq // tk, body, 0)
    out = acc_ref[...] / l_ref[...]
    o_ref[0] = jnp.concatenate(
        [out[g * tq:(g + 1) * tq] for g in range(group)], axis=1).astype(BF16)


def _attn_a(q, k, v, *, tq, tk):
    B, S, _ = q.shape
    gw = A_GROUP * HEAD_DIM
    rows = A_GROUP * tq
    kern = functools.partial(_attn_a_kernel, tk=tk, group=A_GROUP)
    return pl.pallas_call(
        kern,
        grid=(B, A_KV_HEADS, S // tq),
        in_specs=[
            pl.BlockSpec((1, tq, gw), lambda b, h, i: (b, i, h)),
            pl.BlockSpec((1, S, HEAD_DIM), lambda b, h, i: (b, 0, h)),
            pl.BlockSpec((1, S, HEAD_DIM), lambda b, h, i: (b, 0, h)),
        ],
        out_specs=pl.BlockSpec((1, tq, gw), lambda b, h, i: (b, i, h)),
        out_shape=jax.ShapeDtypeStruct((B, S, A_WIDTH), BF16),
        scratch_shapes=[
            pltpu.VMEM((rows, LANES), F32),
            pltpu.VMEM((rows, LANES), F32),
            pltpu.VMEM((rows, HEAD_DIM), F32),
        ],
        compiler_params=_compiler_params(3),
        name="attn_gqa",
    )(q, k, v)


def _attn_b_kernel(q_ref, k_ref, v_ref, lq1_ref, lk1_ref, lq2_ref, lk2_ref, sg_ref,
                   o_ref, m_ref, l_ref, acc_ref, *, tk, lam_init):
    tq = q_ref.shape[1]
    seq = k_ref.shape[1]
    q = q_ref[0]
    q1 = q[:, :HEAD_DIM]
    q2 = q[:, HEAD_DIM:]
    m_ref[...] = jnp.full(m_ref.shape, -jnp.inf, F32)
    l_ref[...] = jnp.zeros(l_ref.shape, F32)
    acc_ref[...] = jnp.zeros(acc_ref.shape, F32)
    dn = (((1,), (1,)), ((), ()))

    def body(c, carry):
        start = pl.multiple_of(c * tk, tk)
        k = k_ref[0, pl.ds(start, tk), :]
        v = v_ref[0, pl.ds(start, tk), :]
        s1 = lax.dot_general(q1, k[:, :HEAD_DIM], dn, preferred_element_type=F32)
        s2 = lax.dot_general(q2, k[:, HEAD_DIM:], dn, preferred_element_type=F32)
        s = jnp.concatenate([s1, s2], axis=0)
        _softmax_step(s, v, m_ref, l_ref, acc_ref, tk)
        return carry

    lax.fori_loop(0, seq // tk, body, 0)

    lam = (jnp.exp(jnp.sum(lq1_ref[...] * lk1_ref[...], axis=-1, keepdims=True))
           - jnp.exp(jnp.sum(lq2_ref[...] * lk2_ref[...], axis=-1, keepdims=True)) + lam_init)
    dv = acc_ref.shape[1]
    out = acc_ref[...] / jnp.tile(l_ref[...], (1, dv // LANES))
    o = out[:tq] - lam * out[tq:]
    ms = jnp.mean(o * o, axis=-1, keepdims=True)
    o = o * lax.rsqrt(ms + EPS) * sg_ref[...] * (1.0 - lam_init)
    o_ref[0] = o.astype(BF16)


def _attn_b(q, k, v, lq1, lk1, lq2, lk2, subln_g, *, tq, tk, lam_init):
    B, S, _ = q.shape
    hw = 2 * HEAD_DIM
    rows = 2 * tq
    kern = functools.partial(_attn_b_kernel, tk=tk, lam_init=lam_init)
    vec = lambda a: a.reshape(1, -1)
    small = lambda n: pl.BlockSpec((1, n), lambda b, h, i: (0, 0))
    return pl.pallas_call(
        kern,
        grid=(B, B_HEADS, S // tq),
        in_specs=[
            pl.BlockSpec((1, tq, hw), lambda b, h, i: (b, i, h)),
            pl.BlockSpec((1, S, hw), lambda b, h, i: (b, 0, h)),
            pl.BlockSpec((1, S, hw), lambda b, h, i: (b, 0, h)),
            small(HEAD_DIM), small(HEAD_DIM), small(HEAD_DIM), small(HEAD_DIM), small(hw),
        ],
        out_specs=pl.BlockSpec((1, tq, hw), lambda b, h, i: (b, i, h)),
        out_shape=jax.ShapeDtypeStruct((B, S, B_WIDTH), BF16),
        scratch_shapes=[
            pltpu.VMEM((rows, LANES), F32),
            pltpu.VMEM((rows, LANES), F32),
            pltpu.VMEM((rows, hw), F32),
        ],
        compiler_params=_compiler_params(3),
        name="attn_diff",
    )(q, k, v, vec(lq1), vec(lk1), vec(lq2), vec(lk2), vec(subln_g))


def _outproj_kernel(o_ref, z_ref, w_ref, x_ref, out_ref):
    z = z_ref[...].astype(F32)
    gated = (o_ref[...].astype(F32) * (z * jax.nn.sigmoid(z))).astype(BF16)
    out_ref[...] = x_ref[...] + jnp.dot(gated, w_ref[...], preferred_element_type=F32)


def _outproj(o, z, w, x, *, tm):
    T, D = x.shape
    K = o.shape[1]
    return pl.pallas_call(
        _outproj_kernel,
        grid=(T // tm,),
        in_specs=[
            pl.BlockSpec((tm, K), lambda i: (i, 0)),
            pl.BlockSpec((tm, K), lambda i: (i, 0)),
            pl.BlockSpec((K, D), lambda i: (0, 0)),
            pl.BlockSpec((tm, D), lambda i: (i, 0)),
        ],
        out_specs=pl.BlockSpec((tm, D), lambda i: (i, 0)),
        out_shape=jax.ShapeDtypeStruct((T, D), F32),
        compiler_params=_compiler_params(1),
        name="outproj",
    )(o, z, w, x)


def _rope_cos_sin(pos, dim):
    inv = 1.0 / (ROPE_THETA ** (jnp.arange(0, dim, 2, dtype=F32) / dim))
    ang = pos[:, None] * inv[None, :]
    return jnp.cos(ang), jnp.sin(ang)


A_SHIFTS = (HEAD_DIM - HEAD_DIM // 4, HEAD_DIM // 4)
B_SHIFTS = (HEAD_DIM // 2,)


def _axial_tables(seq, gain, scale):
    n_rows = seq // GRID_W
    row = jnp.repeat(jnp.arange(n_rows, dtype=F32), GRID_W)
    col = jnp.tile(jnp.arange(GRID_W, dtype=F32), n_rows)
    half = HEAD_DIM // 2
    cr, sr = _rope_cos_sin(row, half)
    cc, sc = _rope_cos_sin(col, half)
    zeros = jnp.zeros_like(sr)
    cos = jnp.concatenate([cr, cr, cc, cc], axis=1)
    s_lo = jnp.concatenate([-sr, zeros, -sc, zeros], axis=1)
    s_hi = jnp.concatenate([zeros, sr, zeros, sc], axis=1)
    q = HEAD_DIM // 4
    g = gain.astype(F32)
    t0 = cos * g[None, :]
    t1 = s_lo * jnp.roll(g, -q)[None, :]
    t2 = s_hi * jnp.roll(g, q)[None, :]
    return jnp.stack([t0, t1, t2]) * scale


def _rope1d_tables(seq, gain, scale):
    pos = jnp.arange(seq, dtype=F32)
    c, s = _rope_cos_sin(pos, HEAD_DIM)
    g = gain.astype(F32)
    t0 = jnp.concatenate([c, c], axis=1) * g[None, :]
    t1 = jnp.concatenate([-s, s], axis=1) * jnp.roll(g, HEAD_DIM // 2)[None, :]
    return jnp.stack([t0, t1]) * scale


def _tile(n, pref):
    return pref if n % pref == 0 else n


def _gqa_layer(xf, B, S, norm_g, w_in, q_g, k_g, w_out, *, tiles):
    qtab = _axial_tables(S, q_g, HEAD_DIM ** -0.5)
    ktab = _axial_tables(S, k_g, 1.0)
    q, k, v, z = _inproj(xf, norm_g, w_in, qtab, ktab,
                         widths=(A_WIDTH, A_KV_WIDTH, A_KV_WIDTH, A_WIDTH), shifts=A_SHIFTS,
                         seq=S, tm=_tile(S, tiles["proj_tm"]), tn=tiles["proj_tn"])
    o = _attn_a(q.reshape(B, S, A_WIDTH), k.reshape(B, S, A_KV_WIDTH), v.reshape(B, S, A_KV_WIDTH),
                tq=_tile(S, tiles["a_tq"]), tk=_tile(S, tiles["a_tk"]))
    return _outproj(o.reshape(B * S, A_WIDTH), z, w_out, xf, tm=_tile(B * S, tiles["out_tm"]))


def _diff_layer(xf, B, S, layer_idx, norm_g, w_in, q_g, k_g, lq1, lk1, lq2, lk2, subln_g, w_out, *, tiles):
    lam_init = 0.8 - 0.6 * math.exp(-0.3 * layer_idx)
    qtab = _rope1d_tables(S, q_g, HEAD_DIM ** -0.5)
    ktab = _rope1d_tables(S, k_g, 1.0)
    q, k, v, z = _inproj(xf, norm_g, w_in, qtab, ktab,
                         widths=(B_WIDTH,) * 4, shifts=B_SHIFTS,
                         seq=S, tm=_tile(S, tiles["proj_tm"]), tn=tiles["proj_tn"])
    o = _attn_b(q.reshape(B, S, B_WIDTH), k.reshape(B, S, B_WIDTH), v.reshape(B, S, B_WIDTH),
                lq1, lk1, lq2, lk2, subln_g,
                tq=_tile(S, tiles["b_tq"]), tk=_tile(S, tiles["b_tk"]), lam_init=lam_init)
    return _outproj(o.reshape(B * S, B_WIDTH), z, w_out, xf, tm=_tile(B * S, tiles["out_tm"]))


TILES = dict(proj_tm=1024, proj_tn=512, a_tq=256, a_tk=512, b_tq=512, b_tk=512, out_tm=512)


def _trunk(x, params, tiles):
    (a_norm, a_w_in, a_q_norm, a_k_norm, a_w_out,
     b_norm, b_w_in, b_q_norm, b_k_norm, b_lq1, b_lk1, b_lq2, b_lk2, b_subln, b_w_out) = params
    B, S, D = x.shape
    xf = x.reshape(B * S, D)
    for i in range(DEPTH):
        j = i // 2
        if i % 2 == 0:
            xf = _gqa_layer(xf, B, S, a_norm[j], a_w_in[j], a_q_norm[j], a_k_norm[j], a_w_out[j],
                            tiles=tiles)
        else:
            xf = _diff_layer(xf, B, S, i, b_norm[j], b_w_in[j], b_q_norm[j], b_k_norm[j],
                             b_lq1[j], b_lk1[j], b_lq2[j], b_lk2[j], b_subln[j], b_w_out[j],
                             tiles=tiles)
    return xf.reshape(B, S, D)


def kernel(x_prompt, x_sample, a_norm, a_w_in, a_q_norm, a_k_norm, a_w_out, b_norm, b_w_in, b_q_norm, b_k_norm, b_lambda_q1, b_lambda_k1, b_lambda_q2, b_lambda_k2, b_subln, b_w_out):
    params = (a_norm, a_w_in.astype(BF16), a_q_norm, a_k_norm, a_w_out.astype(BF16),
              b_norm, b_w_in.astype(BF16), b_q_norm, b_k_norm,
              b_lambda_q1, b_lambda_k1, b_lambda_q2, b_lambda_k2, b_subln, b_w_out.astype(BF16))
    y_prompt = _trunk(x_prompt, params, TILES)
    y_sample = _trunk(x_sample, params, TILES)
    return (y_prompt, y_sample)
```

```python
import functools
import math

import jax
import jax.numpy as jnp
from jax import lax
from jax.experimental import pallas as pl
from jax.experimental.pallas import tpu as pltpu

D_MODEL = 2048
DEPTH = 4
GRID_W = 64
HEAD_DIM = 128
ROPE_THETA = 10000.0
EPS = 1e-6
A_HEADS = D_MODEL // HEAD_DIM
A_KV_HEADS = A_HEADS // 4
A_GROUP = A_HEADS // A_KV_HEADS
A_WIDTH = A_HEADS * HEAD_DIM
A_KV_WIDTH = A_KV_HEADS * HEAD_DIM
B_HEADS = D_MODEL // (2 * HEAD_DIM)
B_WIDTH = B_HEADS * 2 * HEAD_DIM

LANES = 128
VMEM_LIMIT_BYTES = 56 * 1024 * 1024

F32 = jnp.float32
BF16 = jnp.bfloat16


def _compiler_params(n_axes, flags=None):
    return pltpu.CompilerParams(
        dimension_semantics=("arbitrary",) * n_axes,
        vmem_limit_bytes=VMEM_LIMIT_BYTES,
        flags=flags,
    )


ATTN_FLAGS = None
SOFTMAX_ROW_BLOCK = 256


def _inproj_kernel(x_ref, g_ref, w_ref, qtab_ref, ktab_ref,
                   q_out, k_out, v_out, z_out, h_ref, *, nq, nk, nv, shifts, tn):
    j = pl.program_id(1)

    @pl.when(j == 0)
    def _():
        x = x_ref[...]
        ms = jnp.mean(x * x, axis=-1, keepdims=True)
        h_ref[...] = (x * lax.rsqrt(ms + EPS) * g_ref[...]).astype(BF16)

    y = jnp.dot(h_ref[...], w_ref[...], preferred_element_type=F32)

    def norm_rope(tab_ref):
        outs = []
        for hh in range(tn // HEAD_DIM):
            yh = y[:, hh * HEAD_DIM:(hh + 1) * HEAD_DIM]
            r = lax.rsqrt(jnp.mean(yh * yh, axis=-1, keepdims=True) + EPS)
            acc = yh * tab_ref[0]
            for t, sh in enumerate(shifts):
                acc = acc + pltpu.roll(yh, sh, 1) * tab_ref[1 + t]
            outs.append(acc * r)
        return jnp.concatenate(outs, axis=1)

    @pl.when(j < nq)
    def _():
        q_out[...] = norm_rope(qtab_ref).astype(BF16)

    @pl.when((j >= nq) & (j < nq + nk))
    def _():
        k_out[...] = norm_rope(ktab_ref).astype(BF16)

    @pl.when((j >= nq + nk) & (j < nq + nk + nv))
    def _():
        v_out[...] = y.astype(BF16)

    @pl.when(j >= nq + nk + nv)
    def _():
        z_out[...] = y.astype(BF16)


def _inproj(x, norm_g, w, qtab, ktab, *, widths, shifts, seq, tm, tn):
    T, D = x.shape
    wq, wk, wv, wz = widths
    nq, nk, nv, nz = wq // tn, wk // tn, wv // tn, wz // tn
    n_tab = qtab.shape[0]
    pos_blocks = seq // tm
    grid = (T // tm, nq + nk + nv + nz)

    def clamp(j, lo, n):
        return jnp.clip(j - lo, 0, n - 1)

    kern = functools.partial(_inproj_kernel, nq=nq, nk=nk, nv=nv, shifts=shifts, tn=tn)
    return pl.pallas_call(
        kern,
        grid=grid,
        in_specs=[
            pl.BlockSpec((tm, D), lambda i, j: (i, 0)),
            pl.BlockSpec((1, D), lambda i, j: (0, 0)),
            pl.BlockSpec((D, tn), lambda i, j: (0, j)),
            pl.BlockSpec((n_tab, tm, LANES), lambda i, j: (0, i % pos_blocks, 0)),
            pl.BlockSpec((n_tab, tm, LANES), lambda i, j: (0, i % pos_blocks, 0)),
        ],
        out_specs=[
            pl.BlockSpec((tm, tn), lambda i, j: (i, clamp(j, 0, nq))),
            pl.BlockSpec((tm, tn), lambda i, j: (i, clamp(j, nq, nk))),
            pl.BlockSpec((tm, tn), lambda i, j: (i, clamp(j, nq + nk, nv))),
            pl.BlockSpec((tm, tn), lambda i, j: (i, clamp(j, nq + nk + nv, nz))),
        ],
        out_shape=[
            jax.ShapeDtypeStruct((T, wq), BF16),
            jax.ShapeDtypeStruct((T, wk), BF16),
            jax.ShapeDtypeStruct((T, wv), BF16),
            jax.ShapeDtypeStruct((T, wz), BF16),
        ],
        scratch_shapes=[pltpu.VMEM((tm, D), BF16)],
        compiler_params=_compiler_params(2),
        name="inproj",
    )(x, norm_g.reshape(1, D), w, qtab, ktab)


def _score_stage(s, slot, s_ref, mx_ref):
    s_ref[slot] = s
    mx_ref[slot] = jnp.broadcast_to(jnp.max(s, axis=1, keepdims=True), mx_ref.shape[1:])


def _softmax_stage(slot, v, s_ref, mx_ref, m_ref, l_ref, acc_ref):
    rows, tk = s_ref.shape[1:]
    dv = acc_ref.shape[1]
    for r0 in range(0, rows, SOFTMAX_ROW_BLOCK):
        rb = pl.ds(r0, min(SOFTMAX_ROW_BLOCK, rows - r0))
        m_prev = m_ref[rb, :]
        m_next = jnp.maximum(m_prev, mx_ref[slot, rb, :])
        alpha = jnp.exp2(m_prev - m_next)
        p = jnp.exp2(s_ref[slot, rb, :] - jnp.tile(m_next, (1, tk // LANES)))
        if l_ref is not None:
            part = p[:, :LANES]
            for t in range(1, tk // LANES):
                part = part + p[:, t * LANES:(t + 1) * LANES]
            l_ref[rb, :] = alpha * l_ref[rb, :] + part
        pv = jnp.dot(p.astype(BF16), v, preferred_element_type=F32)
        acc_ref[rb, :] = acc_ref[rb, :] * jnp.tile(alpha, (1, dv // LANES)) + pv
        m_ref[rb, :] = m_next


def _pipelined_kv_loop(n_chunks, score_stage, softmax_stage):
    assert n_chunks % 2 == 0

    def pair(c, last):
        score_stage(c + 1, 1)
        softmax_stage(c, 0)
        if not last:
            score_stage(c + 2, 0)
        softmax_stage(c + 1, 1)

    score_stage(0, 0)

    def body(c2, carry):
        pair(2 * c2, False)
        return carry

    lax.fori_loop(0, n_chunks // 2 - 1, body, 0)
    pair(n_chunks - 2, True)


def _attn_a_kernel(q_ref, k_ref, v_ref, o_ref, s_ref, mx_ref, m_ref, acc_ref, *, group):
    tq = q_ref.shape[1]
    seq = k_ref.shape[1]
    tk = s_ref.shape[2]
    q = q_ref[0]
    qs = jnp.concatenate([q[:, g * HEAD_DIM:(g + 1) * HEAD_DIM] for g in range(group)], axis=0)
    m_ref[...] = jnp.full(m_ref.shape, -jnp.inf, F32)
    acc_ref[...] = jnp.zeros(acc_ref.shape, F32)
    ones = jnp.ones((tk, LANES), BF16)

    def score_stage(c, slot):
        k = k_ref[0, pl.ds(pl.multiple_of(c * tk, tk), tk), :]
        s = lax.dot_general(qs, k, (((1,), (1,)), ((), ())), preferred_element_type=F32)
        _score_stage(s, slot, s_ref, mx_ref)

    def softmax_stage(c, slot):
        v = v_ref[0, pl.ds(pl.multiple_of(c * tk, tk), tk), :]
        v_ext = jnp.concatenate([v, ones], axis=1)
        _softmax_stage(slot, v_ext, s_ref, mx_ref, m_ref, None, acc_ref)

    _pipelined_kv_loop(seq // tk, score_stage, softmax_stage)
    acc = acc_ref[...]
    out = acc[:, :HEAD_DIM] / acc[:, HEAD_DIM:]
    o_ref[0] = jnp.concatenate(
        [out[g * tq:(g + 1) * tq] for g in range(group)], axis=1).astype(BF16)


def _attn_a(q, k, v, *, tq, tk):
    B, S, _ = q.shape
    gw = A_GROUP * HEAD_DIM
    rows = A_GROUP * tq
    kern = functools.partial(_attn_a_kernel, group=A_GROUP)
    return pl.pallas_call(
        kern,
        grid=(B, A_KV_HEADS, S // tq),
        in_specs=[
            pl.BlockSpec((1, tq, gw), lambda b, h, i: (b, i, h)),
            pl.BlockSpec((1, S, HEAD_DIM), lambda b, h, i: (b, 0, h)),
            pl.BlockSpec((1, S, HEAD_DIM), lambda b, h, i: (b, 0, h)),
        ],
        out_specs=pl.BlockSpec((1, tq, gw), lambda b, h, i: (b, i, h)),
        out_shape=jax.ShapeDtypeStruct((B, S, A_WIDTH), BF16),
        scratch_shapes=[
            pltpu.VMEM((2, rows, tk), F32),
            pltpu.VMEM((2, rows, LANES), F32),
            pltpu.VMEM((rows, LANES), F32),
            pltpu.VMEM((rows, 2 * HEAD_DIM), F32),
        ],
        compiler_params=_compiler_params(3, ATTN_FLAGS),
        name="attn_gqa",
    )(q, k, v)


def _attn_b_kernel(q_ref, k_ref, v_ref, lq1_ref, lk1_ref, lq2_ref, lk2_ref, sg_ref,
                   o_ref, s_ref, mx_ref, m_ref, l_ref, acc_ref, *, lam_init):
    tq = q_ref.shape[1]
    seq = k_ref.shape[1]
    tk = s_ref.shape[2]
    q = q_ref[0]
    q1 = q[:, :HEAD_DIM]
    q2 = q[:, HEAD_DIM:]
    m_ref[...] = jnp.full(m_ref.shape, -jnp.inf, F32)
    l_ref[...] = jnp.zeros(l_ref.shape, F32)
    acc_ref[...] = jnp.zeros(acc_ref.shape, F32)
    dn = (((1,), (1,)), ((), ()))

    def score_stage(c, slot):
        k = k_ref[0, pl.ds(pl.multiple_of(c * tk, tk), tk), :]
        s1 = lax.dot_general(q1, k[:, :HEAD_DIM], dn, preferred_element_type=F32)
        s2 = lax.dot_general(q2, k[:, HEAD_DIM:], dn, preferred_element_type=F32)
        _score_stage(jnp.concatenate([s1, s2], axis=0), slot, s_ref, mx_ref)

    def softmax_stage(c, slot):
        v = v_ref[0, pl.ds(pl.multiple_of(c * tk, tk), tk), :]
        _softmax_stage(slot, v, s_ref, mx_ref, m_ref, l_ref, acc_ref)

    _pipelined_kv_loop(seq // tk, score_stage, softmax_stage)

    lam = (jnp.exp(jnp.sum(lq1_ref[...] * lk1_ref[...], axis=-1, keepdims=True))
           - jnp.exp(jnp.sum(lq2_ref[...] * lk2_ref[...], axis=-1, keepdims=True)) + lam_init)
    dv = acc_ref.shape[1]
    l = jnp.sum(l_ref[...], axis=1, keepdims=True)
    out = acc_ref[...] / l
    o = out[:tq] - lam * out[tq:]
    ms = jnp.mean(o * o, axis=-1, keepdims=True)
    o = o * lax.rsqrt(ms + EPS) * sg_ref[...] * (1.0 - lam_init)
    o_ref[0] = o.astype(BF16)


def _attn_b(q, k, v, lq1, lk1, lq2, lk2, subln_g, *, tq, tk, lam_init):
    B, S, _ = q.shape
    hw = 2 * HEAD_DIM
    rows = 2 * tq
    kern = functools.partial(_attn_b_kernel, lam_init=lam_init)
    vec = lambda a: a.reshape(1, -1)
    small = lambda n: pl.BlockSpec((1, n), lambda b, h, i: (0, 0))
    return pl.pallas_call(
        kern,
        grid=(B, B_HEADS, S // tq),
        in_specs=[
            pl.BlockSpec((1, tq, hw), lambda b, h, i: (b, i, h)),
            pl.BlockSpec((1, S, hw), lambda b, h, i: (b, 0, h)),
            pl.BlockSpec((1, S, hw), lambda b, h, i: (b, 0, h)),
            small(HEAD_DIM), small(HEAD_DIM), small(HEAD_DIM), small(HEAD_DIM), small(hw),
        ],
        out_specs=pl.BlockSpec((1, tq, hw), lambda b, h, i: (b, i, h)),
        out_shape=jax.ShapeDtypeStruct((B, S, B_WIDTH), BF16),
        scratch_shapes=[
            pltpu.VMEM((2, rows, tk), F32),
            pltpu.VMEM((2, rows, LANES), F32),
            pltpu.VMEM((rows, LANES), F32),
            pltpu.VMEM((rows, LANES), F32),
            pltpu.VMEM((rows, hw), F32),
        ],
        compiler_params=_compiler_params(3, ATTN_FLAGS),
        name="attn_diff",
    )(q, k, v, vec(lq1), vec(lk1), vec(lq2), vec(lk2), vec(subln_g))


def _outproj_kernel(o_ref, z_ref, w_ref, x_ref, out_ref):
    z = z_ref[...].astype(F32)
    gated = (o_ref[...].astype(F32) * (z * jax.nn.sigmoid(z))).astype(BF16)
    out_ref[...] = x_ref[...] + jnp.dot(gated, w_ref[...], preferred_element_type=F32)


def _outproj(o, z, w, x, *, tm):
    T, D = x.shape
    K = o.shape[1]
    return pl.pallas_call(
        _outproj_kernel,
        grid=(T // tm,),
        in_specs=[
            pl.BlockSpec((tm, K), lambda i: (i, 0)),
            pl.BlockSpec((tm, K), lambda i: (i, 0)),
            pl.BlockSpec((K, D), lambda i: (0, 0)),
            pl.BlockSpec((tm, D), lambda i: (i, 0)),
        ],
        out_specs=pl.BlockSpec((tm, D), lambda i: (i, 0)),
        out_shape=jax.ShapeDtypeStruct((T, D), F32),
        compiler_params=_compiler_params(1),
        name="outproj",
    )(o, z, w, x)


def _rope_cos_sin(pos, dim):
    inv = 1.0 / (ROPE_THETA ** (jnp.arange(0, dim, 2, dtype=F32) / dim))
    ang = pos[:, None] * inv[None, :]
    return jnp.cos(ang), jnp.sin(ang)


Q_SCALE_LOG2 = HEAD_DIM ** -0.5 * math.log2(math.e)

A_SHIFTS = (HEAD_DIM - HEAD_DIM // 4, HEAD_DIM // 4)
B_SHIFTS = (HEAD_DIM // 2,)


def _axial_tables(seq, gain, scale):
    n_rows = seq // GRID_W
    row = jnp.repeat(jnp.arange(n_rows, dtype=F32), GRID_W)
    col = jnp.tile(jnp.arange(GRID_W, dtype=F32), n_rows)
    half = HEAD_DIM // 2
    cr, sr = _rope_cos_sin(row, half)
    cc, sc = _rope_cos_sin(col, half)
    zeros = jnp.zeros_like(sr)
    cos = jnp.concatenate([cr, cr, cc, cc], axis=1)
    s_lo = jnp.concatenate([-sr, zeros, -sc, zeros], axis=1)
    s_hi = jnp.concatenate([zeros, sr, zeros, sc], axis=1)
    q = HEAD_DIM // 4
    g = gain.astype(F32)
    t0 = cos * g[None, :]
    t1 = s_lo * jnp.roll(g, -q)[None, :]
    t2 = s_hi * jnp.roll(g, q)[None, :]
    return jnp.stack([t0, t1, t2]) * scale


def _rope1d_tables(seq, gain, scale):
    pos = jnp.arange(seq, dtype=F32)
    c, s = _rope_cos_sin(pos, HEAD_DIM)
    g = gain.astype(F32)
    t0 = jnp.concatenate([c, c], axis=1) * g[None, :]
    t1 = jnp.concatenate([-s, s], axis=1) * jnp.roll(g, HEAD_DIM // 2)[None, :]
    return jnp.stack([t0, t1]) * scale


def _tile(n, pref):
    return pref if n % pref == 0 else n


def _gqa_layer(xf, B, S, norm_g, w_in, q_g, k_g, w_out, *, tiles):
    qtab = _axial_tables(S, q_g, Q_SCALE_LOG2)
    ktab = _axial_tables(S, k_g, 1.0)
    q, k, v, z = _inproj(xf, norm_g, w_in, qtab, ktab,
                         widths=(A_WIDTH, A_KV_WIDTH, A_KV_WIDTH, A_WIDTH), shifts=A_SHIFTS,
                         seq=S, tm=_tile(S, tiles["proj_tm"]), tn=tiles["proj_tn"])
    o = _attn_a(q.reshape(B, S, A_WIDTH), k.reshape(B, S, A_KV_WIDTH), v.reshape(B, S, A_KV_WIDTH),
                tq=_tile(S, tiles["a_tq"]), tk=_tile(S, tiles["a_tk"]))
    return _outproj(o.reshape(B * S, A_WIDTH), z, w_out, xf, tm=_tile(B * S, tiles["out_tm"]))


def _diff_layer(xf, B, S, layer_idx, norm_g, w_in, q_g, k_g, lq1, lk1, lq2, lk2, subln_g, w_out, *, tiles):
    lam_init = 0.8 - 0.6 * math.exp(-0.3 * layer_idx)
    qtab = _rope1d_tables(S, q_g, Q_SCALE_LOG2)
    ktab = _rope1d_tables(S, k_g, 1.0)
    q, k, v, z = _inproj(xf, norm_g, w_in, qtab, ktab,
                         widths=(B_WIDTH,) * 4, shifts=B_SHIFTS,
                         seq=S, tm=_tile(S, tiles["proj_tm"]), tn=tiles["proj_tn"])
    o = _attn_b(q.reshape(B, S, B_WIDTH), k.reshape(B, S, B_WIDTH), v.reshape(B, S, B_WIDTH),
                lq1, lk1, lq2, lk2, subln_g,
                tq=_tile(S, tiles["b_tq"]), tk=_tile(S, tiles["b_tk"]), lam_init=lam_init)
    return _outproj(o.reshape(B * S, B_WIDTH), z, w_out, xf, tm=_tile(B * S, tiles["out_tm"]))


TILES = dict(proj_tm=1024, proj_tn=512, a_tq=256, a_tk=1024, b_tq=512, b_tk=1024, out_tm=512)


def _trunk(x, params, tiles):
    (a_norm, a_w_in, a_q_norm, a_k_norm, a_w_out,
     b_norm, b_w_in, b_q_norm, b_k_norm, b_lq1, b_lk1, b_lq2, b_lk2, b_subln, b_w_out) = params
    B, S, D = x.shape
    xf = x.reshape(B * S, D)
    for i in range(DEPTH):
        j = i // 2
        if i % 2 == 0:
            xf = _gqa_layer(xf, B, S, a_norm[j], a_w_in[j], a_q_norm[j], a_k_norm[j], a_w_out[j],
                            tiles=tiles)
        else:
            xf = _diff_layer(xf, B, S, i, b_norm[j], b_w_in[j], b_q_norm[j], b_k_norm[j],
                             b_lq1[j], b_lk1[j], b_lq2[j], b_lk2[j], b_subln[j], b_w_out[j],
                             tiles=tiles)
    return xf.reshape(B, S, D)


def kernel(x_prompt, x_sample, a_norm, a_w_in, a_q_norm, a_k_norm, a_w_out, b_norm, b_w_in, b_q_norm, b_k_norm, b_lambda_q1, b_lambda_k1, b_lambda_q2, b_lambda_k2, b_subln, b_w_out):
    params = (a_norm, a_w_in.astype(BF16), a_q_norm, a_k_norm, a_w_out.astype(BF16),
              b_norm, b_w_in.astype(BF16), b_q_norm, b_k_norm,
              b_lambda_q1, b_lambda_k1, b_lambda_q2, b_lambda_k2, b_subln, b_w_out.astype(BF16))
    y_prompt = _trunk(x_prompt, params, TILES)
    y_sample = _trunk(x_sample, params, TILES)
    return (y_prompt, y_sample)
```

```python
import functools
import math

import jax
import jax.numpy as jnp
from jax import lax
from jax.experimental import pallas as pl
from jax.experimental.pallas import tpu as pltpu

D_MODEL = 2048
DEPTH = 4
GRID_W = 64
HEAD_DIM = 128
ROPE_THETA = 10000.0
EPS = 1e-6
A_HEADS = D_MODEL // HEAD_DIM
A_KV_HEADS = A_HEADS // 4
A_GROUP = A_HEADS // A_KV_HEADS
A_WIDTH = A_HEADS * HEAD_DIM
A_KV_WIDTH = A_KV_HEADS * HEAD_DIM
B_HEADS = D_MODEL // (2 * HEAD_DIM)
B_WIDTH = B_HEADS * 2 * HEAD_DIM

LANES = 128
VMEM_LIMIT_BYTES = 56 * 1024 * 1024
ROPE_SHIFT = HEAD_DIM // 2

F32 = jnp.float32
BF16 = jnp.bfloat16


def _compiler_params(n_axes):
    return pltpu.CompilerParams(
        dimension_semantics=("arbitrary",) * n_axes,
        vmem_limit_bytes=VMEM_LIMIT_BYTES,
    )


SOFTMAX_ROW_BLOCK = 256


def _rms(x, g):
    ms = jnp.mean(x * x, axis=-1, keepdims=True)
    return (x * lax.rsqrt(ms + EPS) * g).astype(BF16)


def _rmsnorm_kernel(x_ref, g_ref, h_ref):
    h_ref[...] = _rms(x_ref[...], g_ref[...])


def _rmsnorm(x, g, *, tm):
    T, D = x.shape
    return pl.pallas_call(
        _rmsnorm_kernel,
        grid=(T // tm,),
        in_specs=[pl.BlockSpec((tm, D), lambda i: (i, 0)), pl.BlockSpec((1, D), lambda i: (0, 0))],
        out_specs=pl.BlockSpec((tm, D), lambda i: (i, 0)),
        out_shape=jax.ShapeDtypeStruct((T, D), BF16),
        compiler_params=_compiler_params(1),
        name="rmsnorm",
    )(x, g.reshape(1, D))


def _proj_kernel(h_ref, w_ref, tab_ref, out_ref, *, tn, n_rope_blocks):
    h = h_ref[...]
    for j in range(out_ref.shape[1] // tn):
        cols = pl.ds(j * tn, tn)
        y = jnp.dot(h, w_ref[:, cols], preferred_element_type=F32)
        if j < n_rope_blocks:
            t0 = tab_ref[0]
            t1 = tab_ref[1]
            outs = []
            for hh in range(tn // HEAD_DIM):
                yh = y[:, hh * HEAD_DIM:(hh + 1) * HEAD_DIM]
                inv = lax.rsqrt(jnp.mean(yh * yh, axis=-1, keepdims=True) + EPS)
                outs.append((yh * t0 + pltpu.roll(yh, ROPE_SHIFT, 1) * t1) * inv)
            y = jnp.concatenate(outs, axis=1)
        out_ref[:, cols] = y.astype(BF16)


def _proj(h, w, tabs, *, rope_width, seq, tm, tn, name):
    T, D = h.shape
    N = w.shape[1]
    pos_blocks = seq // tm
    return pl.pallas_call(
        functools.partial(_proj_kernel, tn=tn, n_rope_blocks=rope_width // tn),
        grid=(T // tm,),
        in_specs=[
            pl.BlockSpec((tm, D), lambda i: (i, 0)),
            pl.BlockSpec((D, N), lambda i: (0, 0), pipeline_mode=pl.Buffered(1)),
            pl.BlockSpec((2, tm, LANES), lambda i: (0, i % pos_blocks, 0)),
        ],
        out_specs=pl.BlockSpec((tm, N), lambda i: (i, 0)),
        out_shape=jax.ShapeDtypeStruct((T, N), BF16),
        compiler_params=_compiler_params(1),
        name=name,
    )(h, w, tabs)


def _score_stage(s, slot, s_ref, mx_ref):
    s_ref[slot] = s
    mx_ref[slot] = jnp.broadcast_to(jnp.max(s, axis=1, keepdims=True), mx_ref.shape[1:])


def _softmax_stage(slot, v, s_ref, mx_ref, m_ref, l_ref, acc_ref):
    rows, tk = s_ref.shape[1:]
    dv = acc_ref.shape[1]
    for r0 in range(0, rows, SOFTMAX_ROW_BLOCK):
        rb = pl.ds(r0, min(SOFTMAX_ROW_BLOCK, rows - r0))
        m_prev = m_ref[rb, :]
        m_next = jnp.maximum(m_prev, mx_ref[slot, rb, :])
        alpha = jnp.exp2(m_prev - m_next)
        p = jnp.exp2(s_ref[slot, rb, :] - jnp.tile(m_next, (1, tk // LANES)))
        if l_ref is not None:
            part = p[:, :LANES]
            for t in range(1, tk // LANES):
                part = part + p[:, t * LANES:(t + 1) * LANES]
            l_ref[rb, :] = alpha * l_ref[rb, :] + part
        pv = jnp.dot(p.astype(BF16), v, preferred_element_type=F32)
        acc_ref[rb, :] = acc_ref[rb, :] * jnp.tile(alpha, (1, dv // LANES)) + pv
        m_ref[rb, :] = m_next


def _pipelined_kv_loop(n_chunks, score_stage, softmax_stage):
    assert n_chunks % 2 == 0

    def pair(c, last):
        score_stage(c + 1, 1)
        softmax_stage(c, 0)
        if not last:
            score_stage(c + 2, 0)
        softmax_stage(c + 1, 1)

    score_stage(0, 0)

    def body(c2, carry):
        pair(2 * c2, False)
        return carry

    lax.fori_loop(0, n_chunks // 2 - 1, body, 0)
    pair(n_chunks - 2, True)


def _attn_a_kernel(q_ref, k_ref, v_ref, o_ref, s_ref, mx_ref, m_ref, acc_ref, *, group):
    tq = q_ref.shape[1]
    seq = k_ref.shape[1]
    tk = s_ref.shape[2]
    q = q_ref[0]
    qs = jnp.concatenate([q[:, g * HEAD_DIM:(g + 1) * HEAD_DIM] for g in range(group)], axis=0)
    m_ref[...] = jnp.full(m_ref.shape, -jnp.inf, F32)
    acc_ref[...] = jnp.zeros(acc_ref.shape, F32)
    ones = jnp.ones((tk, LANES), BF16)

    def score_stage(c, slot):
        k = k_ref[0, pl.ds(pl.multiple_of(c * tk, tk), tk), :]
        s = lax.dot_general(qs, k, (((1,), (1,)), ((), ())), preferred_element_type=F32)
        _score_stage(s, slot, s_ref, mx_ref)

    def softmax_stage(c, slot):
        v = v_ref[0, pl.ds(pl.multiple_of(c * tk, tk), tk), :]
        v_ext = jnp.concatenate([v, ones], axis=1)
        _softmax_stage(slot, v_ext, s_ref, mx_ref, m_ref, None, acc_ref)

    _pipelined_kv_loop(seq // tk, score_stage, softmax_stage)
    acc = acc_ref[...]
    out = acc[:, :HEAD_DIM] / acc[:, HEAD_DIM:]
    o_ref[0] = jnp.concatenate(
        [out[g * tq:(g + 1) * tq] for g in range(group)], axis=1).astype(BF16)


def _attn_a(qz, kv, *, tq, tk):
    B, S, _ = qz.shape
    gw = A_GROUP * HEAD_DIM
    rows = A_GROUP * tq
    k_off = 0
    v_off = A_KV_WIDTH // HEAD_DIM
    kern = functools.partial(_attn_a_kernel, group=A_GROUP)
    return pl.pallas_call(
        kern,
        grid=(B, A_KV_HEADS, S // tq),
        in_specs=[
            pl.BlockSpec((1, tq, gw), lambda b, h, i: (b, i, h)),
            pl.BlockSpec((1, S, HEAD_DIM), lambda b, h, i: (b, 0, k_off + h)),
            pl.BlockSpec((1, S, HEAD_DIM), lambda b, h, i: (b, 0, v_off + h)),
        ],
        out_specs=pl.BlockSpec((1, tq, gw), lambda b, h, i: (b, i, h)),
        out_shape=jax.ShapeDtypeStruct((B, S, A_WIDTH), BF16),
        scratch_shapes=[
            pltpu.VMEM((2, rows, tk), F32),
            pltpu.VMEM((2, rows, LANES), F32),
            pltpu.VMEM((rows, LANES), F32),
            pltpu.VMEM((rows, 2 * HEAD_DIM), F32),
        ],
        compiler_params=_compiler_params(3),
        name="attn_gqa",
    )(qz, kv, kv)


def _attn_b_kernel(q_ref, k_ref, v_ref, lq1_ref, lk1_ref, lq2_ref, lk2_ref, sg_ref,
                   o_ref, s_ref, mx_ref, m_ref, l_ref, acc_ref, *, lam_init):
    tq = q_ref.shape[1]
    seq = k_ref.shape[1]
    tk = s_ref.shape[2]
    q = q_ref[0]
    q1 = q[:, :HEAD_DIM]
    q2 = q[:, HEAD_DIM:]
    m_ref[...] = jnp.full(m_ref.shape, -jnp.inf, F32)
    l_ref[...] = jnp.zeros(l_ref.shape, F32)
    acc_ref[...] = jnp.zeros(acc_ref.shape, F32)
    dn = (((1,), (1,)), ((), ()))

    def score_stage(c, slot):
        k = k_ref[0, pl.ds(pl.multiple_of(c * tk, tk), tk), :]
        s1 = lax.dot_general(q1, k[:, :HEAD_DIM], dn, preferred_element_type=F32)
        s2 = lax.dot_general(q2, k[:, HEAD_DIM:], dn, preferred_element_type=F32)
        _score_stage(jnp.concatenate([s1, s2], axis=0), slot, s_ref, mx_ref)

    def softmax_stage(c, slot):
        v = v_ref[0, pl.ds(pl.multiple_of(c * tk, tk), tk), :]
        _softmax_stage(slot, v, s_ref, mx_ref, m_ref, l_ref, acc_ref)

    _pipelined_kv_loop(seq // tk, score_stage, softmax_stage)

    lam = (jnp.exp(jnp.sum(lq1_ref[...] * lk1_ref[...], axis=-1, keepdims=True))
           - jnp.exp(jnp.sum(lq2_ref[...] * lk2_ref[...], axis=-1, keepdims=True)) + lam_init)
    l = jnp.sum(l_ref[...], axis=1, keepdims=True)
    out = acc_ref[...] / l
    o = out[:tq] - lam * out[tq:]
    ms = jnp.mean(o * o, axis=-1, keepdims=True)
    o = o * lax.rsqrt(ms + EPS) * sg_ref[...] * (1.0 - lam_init)
    o_ref[0] = o.astype(BF16)


def _attn_b(qz, kv, lq1, lk1, lq2, lk2, subln_g, *, tq, tk, lam_init):
    B, S, _ = qz.shape
    hw = 2 * HEAD_DIM
    rows = 2 * tq
    k_off = 0
    v_off = B_WIDTH // hw
    kern = functools.partial(_attn_b_kernel, lam_init=lam_init)
    vec = lambda a: a.reshape(1, -1)
    small = lambda n: pl.BlockSpec((1, n), lambda b, h, i: (0, 0))
    return pl.pallas_call(
        kern,
        grid=(B, B_HEADS, S // tq),
        in_specs=[
            pl.BlockSpec((1, tq, hw), lambda b, h, i: (b, i, h)),
            pl.BlockSpec((1, S, hw), lambda b, h, i: (b, 0, k_off + h)),
            pl.BlockSpec((1, S, hw), lambda b, h, i: (b, 0, v_off + h)),
            small(HEAD_DIM), small(HEAD_DIM), small(HEAD_DIM), small(HEAD_DIM), small(hw),
        ],
        out_specs=pl.BlockSpec((1, tq, hw), lambda b, h, i: (b, i, h)),
        out_shape=jax.ShapeDtypeStruct((B, S, B_WIDTH), BF16),
        scratch_shapes=[
            pltpu.VMEM((2, rows, tk), F32),
            pltpu.VMEM((2, rows, LANES), F32),
            pltpu.VMEM((rows, LANES), F32),
            pltpu.VMEM((rows, LANES), F32),
            pltpu.VMEM((rows, hw), F32),
        ],
        compiler_params=_compiler_params(3),
        name="attn_diff",
    )(qz, kv, kv, vec(lq1), vec(lk1), vec(lq2), vec(lk2), vec(subln_g))


def _outproj_kernel(o_ref, z_ref, w_ref, x_ref, *rest, emit_h):
    if emit_h:
        g_ref, out_ref, h_ref = rest
    else:
        (out_ref,) = rest
    z = z_ref[...].astype(F32)
    gated = (o_ref[...].astype(F32) * (z * jax.nn.sigmoid(z))).astype(BF16)
    out = x_ref[...] + jnp.dot(gated, w_ref[...], preferred_element_type=F32)
    out_ref[...] = out
    if emit_h:
        h_ref[...] = _rms(out, g_ref[...])


def _outproj(o, qz, w, x, g_next, *, tm):
    T, D = x.shape
    K = o.shape[1]
    emit_h = g_next is not None
    in_specs = [
        pl.BlockSpec((tm, K), lambda i: (i, 0)),
        pl.BlockSpec((tm, K), lambda i: (i, 1)),
        pl.BlockSpec((K, D), lambda i: (0, 0)),
        pl.BlockSpec((tm, D), lambda i: (i, 0)),
    ]
    args = [o, qz, w, x]
    out_specs = [pl.BlockSpec((tm, D), lambda i: (i, 0))]
    out_shape = [jax.ShapeDtypeStruct((T, D), F32)]
    if emit_h:
        in_specs.append(pl.BlockSpec((1, D), lambda i: (0, 0)))
        args.append(g_next.reshape(1, D))
        out_specs.append(pl.BlockSpec((tm, D), lambda i: (i, 0)))
        out_shape.append(jax.ShapeDtypeStruct((T, D), BF16))
    res = pl.pallas_call(
        functools.partial(_outproj_kernel, emit_h=emit_h),
        grid=(T // tm,),
        in_specs=in_specs,
        out_specs=out_specs,
        out_shape=out_shape,
        compiler_params=_compiler_params(1),
        name="outproj",
    )(*args)
    return (res[0], res[1]) if emit_h else (res[0], None)


def _rope_cos_sin(pos, dim):
    inv = 1.0 / (ROPE_THETA ** (jnp.arange(0, dim, 2, dtype=F32) / dim))
    ang = pos[:, None] * inv[None, :]
    return jnp.cos(ang), jnp.sin(ang)


Q_SCALE_LOG2 = HEAD_DIM ** -0.5 * math.log2(math.e)

_Q = HEAD_DIM // 4
AXIAL_PERM = tuple(list(range(0, _Q)) + list(range(2 * _Q, 3 * _Q))
                   + list(range(_Q, 2 * _Q)) + list(range(3 * _Q, 4 * _Q)))


def _tables(cos, sin, gain, scale):
    g = gain.astype(F32)
    t0 = jnp.concatenate([cos, cos], axis=1) * g[None, :]
    t1 = jnp.concatenate([-sin, sin], axis=1) * jnp.roll(g, ROPE_SHIFT)[None, :]
    return jnp.stack([t0, t1]) * scale


def _axial_tables(seq, q_gain, k_gain):
    n_rows = seq // GRID_W
    row = jnp.repeat(jnp.arange(n_rows, dtype=F32), GRID_W)
    col = jnp.tile(jnp.arange(GRID_W, dtype=F32), n_rows)
    half = HEAD_DIM // 2
    cr, sr = _rope_cos_sin(row, half)
    cc, sc = _rope_cos_sin(col, half)
    cos = jnp.concatenate([cr, cc], axis=1)
    sin = jnp.concatenate([sr, sc], axis=1)
    perm = jnp.array(AXIAL_PERM)
    return _tables(cos, sin, q_gain[perm], Q_SCALE_LOG2), _tables(cos, sin, k_gain[perm], 1.0)


def _rope1d_tables(seq, q_gain, k_gain):
    cos, sin = _rope_cos_sin(jnp.arange(seq, dtype=F32), HEAD_DIM)
    return _tables(cos, sin, q_gain, Q_SCALE_LOG2), _tables(cos, sin, k_gain, 1.0)


def _permute_heads(w):
    d, n = w.shape
    return w.reshape(d, n // HEAD_DIM, HEAD_DIM)[:, :, jnp.array(AXIAL_PERM)].reshape(d, n)


def _tile(n, pref):
    return pref if n % pref == 0 else n


def _project(h, w_qz, w_kv, tabs, q_width, k_width, S, tiles):
    qtab, ktab = tabs
    common = dict(seq=S, tm=_tile(S, tiles["proj_tm"]), tn=tiles["proj_tn"])
    qz = _proj(h, w_qz, qtab, rope_width=q_width, name="proj_qz", **common)
    kv = _proj(h, w_kv, ktab, rope_width=k_width, name="proj_kv", **common)
    return qz, kv


def _gqa_layer(xf, h, B, S, w_qz, w_kv, q_g, k_g, w_out, g_next, *, tiles):
    qz, kv = _project(h, w_qz, w_kv, _axial_tables(S, q_g, k_g), A_WIDTH, A_KV_WIDTH, S, tiles)
    o = _attn_a(qz.reshape(B, S, -1), kv.reshape(B, S, -1),
                tq=_tile(S, tiles["a_tq"]), tk=_tile(S, tiles["a_tk"]))
    return _outproj(o.reshape(B * S, A_WIDTH), qz, w_out, xf, g_next,
                    tm=_tile(B * S, tiles["out_tm"]))


def _diff_layer(xf, h, B, S, layer_idx, w_qz, w_kv, q_g, k_g, lq1, lk1, lq2, lk2, subln_g, w_out, g_next,
                *, tiles):
    lam_init = 0.8 - 0.6 * math.exp(-0.3 * layer_idx)
    qz, kv = _project(h, w_qz, w_kv, _rope1d_tables(S, q_g, k_g), B_WIDTH, B_WIDTH, S, tiles)
    o = _attn_b(qz.reshape(B, S, -1), kv.reshape(B, S, -1), lq1, lk1, lq2, lk2, subln_g,
                tq=_tile(S, tiles["b_tq"]), tk=_tile(S, tiles["b_tk"]), lam_init=lam_init)
    return _outproj(o.reshape(B * S, B_WIDTH), qz, w_out, xf, g_next,
                    tm=_tile(B * S, tiles["out_tm"]))


TILES = dict(proj_tm=1024, proj_tn=512, a_tq=256, a_tk=1024, b_tq=512, b_tk=1024, out_tm=512)


def _prepare_weights(a_w_in, a_w_out, b_w_in, b_w_out):
    a = a_w_in.astype(BF16)
    b = b_w_in.astype(BF16)
    cat = lambda x, y: jnp.concatenate([x, y], axis=1)
    k0, v0, z0 = A_WIDTH, A_WIDTH + A_KV_WIDTH, A_WIDTH + 2 * A_KV_WIDTH
    a_qz = [cat(_permute_heads(a[j, :, :k0]), a[j, :, z0:]) for j in range(a.shape[0])]
    a_kv = [cat(_permute_heads(a[j, :, k0:v0]), a[j, :, v0:z0]) for j in range(a.shape[0])]
    b_qz = [cat(b[j, :, :B_WIDTH], b[j, :, 3 * B_WIDTH:]) for j in range(b.shape[0])]
    b_kv = [b[j, :, B_WIDTH:3 * B_WIDTH] for j in range(b.shape[0])]
    return a_qz, a_kv, a_w_out.astype(BF16), b_qz, b_kv, b_w_out.astype(BF16)


def _trunk(x, weights, params, tiles):
    a_qz, a_kv, a_w_out, b_qz, b_kv, b_w_out = weights
    (a_norm, a_q_norm, a_k_norm, b_norm, b_q_norm, b_k_norm,
     b_lq1, b_lk1, b_lq2, b_lk2, b_subln) = params
    B, S, D = x.shape
    xf = x.reshape(B * S, D)
    norm_gain = lambda i: (a_norm if i % 2 == 0 else b_norm)[i // 2]
    h = _rmsnorm(xf, norm_gain(0), tm=_tile(B * S, tiles["out_tm"]))
    for i in range(DEPTH):
        j = i // 2
        g_next = norm_gain(i + 1) if i + 1 < DEPTH else None
        if i % 2 == 0:
            xf, h = _gqa_layer(xf, h, B, S, a_qz[j], a_kv[j], a_q_norm[j], a_k_norm[j], a_w_out[j],
                               g_next, tiles=tiles)
        else:
            xf, h = _diff_layer(xf, h, B, S, i, b_qz[j], b_kv[j], b_q_norm[j], b_k_norm[j],
                                b_lq1[j], b_lk1[j], b_lq2[j], b_lk2[j], b_subln[j], b_w_out[j],
                                g_next, tiles=tiles)
    return xf.reshape(B, S, D)


def kernel(x_prompt, x_sample, a_norm, a_w_in, a_q_norm, a_k_norm, a_w_out, b_norm, b_w_in, b_q_norm, b_k_norm, b_lambda_q1, b_lambda_k1, b_lambda_q2, b_lambda_k2, b_subln, b_w_out):
    weights = _prepare_weights(a_w_in, a_w_out, b_w_in, b_w_out)
    params = (a_norm, a_q_norm, a_k_norm, b_norm, b_q_norm, b_k_norm,
              b_lambda_q1, b_lambda_k1, b_lambda_q2, b_lambda_k2, b_subln)
    y_prompt = _trunk(x_prompt, weights, params, TILES)
    y_sample = _trunk(x_sample, weights, params, TILES)
    return (y_prompt, y_sample)
```

```python
import functools
import math

import jax
import jax.numpy as jnp
from jax import lax
from jax.experimental import pallas as pl
from jax.experimental.pallas import tpu as pltpu

D_MODEL = 2048
DEPTH = 4
GRID_W = 64
HEAD_DIM = 128
ROPE_THETA = 10000.0
EPS = 1e-6
A_HEADS = D_MODEL // HEAD_DIM
A_KV_HEADS = A_HEADS // 4
A_GROUP = A_HEADS // A_KV_HEADS
A_WIDTH = A_HEADS * HEAD_DIM
A_KV_WIDTH = A_KV_HEADS * HEAD_DIM
B_HEADS = D_MODEL // (2 * HEAD_DIM)
B_WIDTH = B_HEADS * 2 * HEAD_DIM

LANES = 128
VMEM_LIMIT_BYTES = 56 * 1024 * 1024
ROPE_SHIFT = HEAD_DIM // 2

F32 = jnp.float32
BF16 = jnp.bfloat16


def _compiler_params(n_axes):
    return pltpu.CompilerParams(
        dimension_semantics=("arbitrary",) * n_axes,
        vmem_limit_bytes=VMEM_LIMIT_BYTES,
    )


SOFTMAX_ROW_BLOCK = 256


def _rms(x, g):
    ms = jnp.mean(x * x, axis=-1, keepdims=True)
    return (x * lax.rsqrt(ms + EPS) * g).astype(BF16)


def _rmsnorm_kernel(x_ref, g_ref, h_ref):
    h_ref[...] = _rms(x_ref[...], g_ref[...])


def _rmsnorm(x, g, *, tm):
    T, D = x.shape
    return pl.pallas_call(
        _rmsnorm_kernel,
        grid=(T // tm,),
        in_specs=[pl.BlockSpec((tm, D), lambda i: (i, 0)), pl.BlockSpec((1, D), lambda i: (0, 0))],
        out_specs=pl.BlockSpec((tm, D), lambda i: (i, 0)),
        out_shape=jax.ShapeDtypeStruct((T, D), BF16),
        compiler_params=_compiler_params(1),
        name="rmsnorm",
    )(x, g.reshape(1, D))


def _proj_kernel(h_ref, w_ref, tab_ref, out_ref, *, tn, n_rope_blocks):
    h = h_ref[...]
    for j in range(out_ref.shape[1] // tn):
        cols = pl.ds(j * tn, tn)
        y = jnp.dot(h, w_ref[:, cols], preferred_element_type=F32)
        if j < n_rope_blocks:
            t0 = tab_ref[0]
            t1 = tab_ref[1]
            outs = []
            for hh in range(tn // HEAD_DIM):
                yh = y[:, hh * HEAD_DIM:(hh + 1) * HEAD_DIM]
                inv = lax.rsqrt(jnp.mean(yh * yh, axis=-1, keepdims=True) + EPS)
                outs.append((yh * t0 + pltpu.roll(yh, ROPE_SHIFT, 1) * t1) * inv)
            y = jnp.concatenate(outs, axis=1)
        out_ref[:, cols] = y.astype(BF16)


def _proj(h, w, tabs, *, rope_width, seq, tm, tn, name):
    T, D = h.shape
    N = w.shape[1]
    pos_blocks = seq // tm
    return pl.pallas_call(
        functools.partial(_proj_kernel, tn=tn, n_rope_blocks=rope_width // tn),
        grid=(T // tm,),
        in_specs=[
            pl.BlockSpec((tm, D), lambda i: (i, 0)),
            pl.BlockSpec((D, N), lambda i: (0, 0), pipeline_mode=pl.Buffered(1)),
            pl.BlockSpec((2, tm, LANES), lambda i: (0, i % pos_blocks, 0)),
        ],
        out_specs=pl.BlockSpec((tm, N), lambda i: (i, 0)),
        out_shape=jax.ShapeDtypeStruct((T, N), BF16),
        compiler_params=_compiler_params(1),
        name=name,
    )(h, w, tabs)


def _score_stage(s, slot, s_ref, mx_ref):
    s_ref[slot] = s
    mx_ref[slot] = jnp.broadcast_to(jnp.max(s, axis=1, keepdims=True), mx_ref.shape[1:])


def _softmax_stage(slot, v, s_ref, mx_ref, m_ref, l_ref, acc_ref):
    rows, tk = s_ref.shape[1:]
    dv = acc_ref.shape[1]
    for r0 in range(0, rows, SOFTMAX_ROW_BLOCK):
        rb = pl.ds(r0, min(SOFTMAX_ROW_BLOCK, rows - r0))
        m_prev = m_ref[rb, :]
        m_next = jnp.maximum(m_prev, mx_ref[slot, rb, :])
        alpha = jnp.exp2(m_prev - m_next)
        p = jnp.exp2(s_ref[slot, rb, :] - jnp.tile(m_next, (1, tk // LANES)))
        if l_ref is not None:
            part = p[:, :LANES]
            for t in range(1, tk // LANES):
                part = part + p[:, t * LANES:(t + 1) * LANES]
            l_ref[rb, :] = alpha * l_ref[rb, :] + part
        pv = jnp.dot(p.astype(BF16), v, preferred_element_type=F32)
        acc_ref[rb, :] = acc_ref[rb, :] * jnp.tile(alpha, (1, dv // LANES)) + pv
        m_ref[rb, :] = m_next


def _pipelined_kv_loop(n_chunks, score_stage, softmax_stage):
    score_stage(0, 0)
    for c in range(n_chunks):
        if c + 1 < n_chunks:
            score_stage(c + 1, (c + 1) % 2)
        softmax_stage(c, c % 2)


def _attn_a_kernel(q_ref, k_ref, v_ref, o_ref, s_ref, mx_ref, m_ref, acc_ref, *, group):
    tq = q_ref.shape[1]
    seq = k_ref.shape[1]
    tk = s_ref.shape[2]
    q = q_ref[0]
    qs = jnp.concatenate([q[:, g * HEAD_DIM:(g + 1) * HEAD_DIM] for g in range(group)], axis=0)
    m_ref[...] = jnp.full(m_ref.shape, -jnp.inf, F32)
    acc_ref[...] = jnp.zeros(acc_ref.shape, F32)
    ones = jnp.ones((tk, LANES), BF16)

    def score_stage(c, slot):
        k = k_ref[0, pl.ds(pl.multiple_of(c * tk, tk), tk), :]
        s = lax.dot_general(qs, k, (((1,), (1,)), ((), ())), preferred_element_type=F32)
        _score_stage(s, slot, s_ref, mx_ref)

    def softmax_stage(c, slot):
        v = v_ref[0, pl.ds(pl.multiple_of(c * tk, tk), tk), :]
        v_ext = jnp.concatenate([v, ones], axis=1)
        _softmax_stage(slot, v_ext, s_ref, mx_ref, m_ref, None, acc_ref)

    _pipelined_kv_loop(seq // tk, score_stage, softmax_stage)
    acc = acc_ref[...]
    out = acc[:, :HEAD_DIM] / acc[:, HEAD_DIM:]
    o_ref[0] = jnp.concatenate(
        [out[g * tq:(g + 1) * tq] for g in range(group)], axis=1).astype(BF16)


def _attn_a(qz, kv, *, tq, tk):
    B, S, _ = qz.shape
    gw = A_GROUP * HEAD_DIM
    rows = A_GROUP * tq
    k_off = 0
    v_off = A_KV_WIDTH // HEAD_DIM
    kern = functools.partial(_attn_a_kernel, group=A_GROUP)
    return pl.pallas_call(
        kern,
        grid=(B, A_KV_HEADS, S // tq),
        in_specs=[
            pl.BlockSpec((1, tq, gw), lambda b, h, i: (b, i, h)),
            pl.BlockSpec((1, S, HEAD_DIM), lambda b, h, i: (b, 0, k_off + h)),
            pl.BlockSpec((1, S, HEAD_DIM), lambda b, h, i: (b, 0, v_off + h)),
        ],
        out_specs=pl.BlockSpec((1, tq, gw), lambda b, h, i: (b, i, h)),
        out_shape=jax.ShapeDtypeStruct((B, S, A_WIDTH), BF16),
        scratch_shapes=[
            pltpu.VMEM((2, rows, tk), F32),
            pltpu.VMEM((2, rows, LANES), F32),
            pltpu.VMEM((rows, LANES), F32),
            pltpu.VMEM((rows, 2 * HEAD_DIM), F32),
        ],
        compiler_params=_compiler_params(3),
        name="attn_gqa",
    )(qz, kv, kv)


def _attn_b_kernel(q_ref, k_ref, v_ref, lq1_ref, lk1_ref, lq2_ref, lk2_ref, sg_ref,
                   o_ref, s_ref, mx_ref, m_ref, l_ref, acc_ref, *, lam_init):
    tq = q_ref.shape[1]
    seq = k_ref.shape[1]
    tk = s_ref.shape[2]
    q = q_ref[0]
    q1 = q[:, :HEAD_DIM]
    q2 = q[:, HEAD_DIM:]
    m_ref[...] = jnp.full(m_ref.shape, -jnp.inf, F32)
    l_ref[...] = jnp.zeros(l_ref.shape, F32)
    acc_ref[...] = jnp.zeros(acc_ref.shape, F32)
    dn = (((1,), (1,)), ((), ()))

    def score_stage(c, slot):
        k = k_ref[0, pl.ds(pl.multiple_of(c * tk, tk), tk), :]
        s1 = lax.dot_general(q1, k[:, :HEAD_DIM], dn, preferred_element_type=F32)
        s2 = lax.dot_general(q2, k[:, HEAD_DIM:], dn, preferred_element_type=F32)
        _score_stage(jnp.concatenate([s1, s2], axis=0), slot, s_ref, mx_ref)

    def softmax_stage(c, slot):
        v = v_ref[0, pl.ds(pl.multiple_of(c * tk, tk), tk), :]
        _softmax_stage(slot, v, s_ref, mx_ref, m_ref, l_ref, acc_ref)

    _pipelined_kv_loop(seq // tk, score_stage, softmax_stage)

    lam = (jnp.exp(jnp.sum(lq1_ref[...] * lk1_ref[...], axis=-1, keepdims=True))
           - jnp.exp(jnp.sum(lq2_ref[...] * lk2_ref[...], axis=-1, keepdims=True)) + lam_init)
    l = jnp.sum(l_ref[...], axis=1, keepdims=True)
    out = acc_ref[...] / l
    o = out[:tq] - lam * out[tq:]
    ms = jnp.mean(o * o, axis=-1, keepdims=True)
    o = o * lax.rsqrt(ms + EPS) * sg_ref[...] * (1.0 - lam_init)
    o_ref[0] = o.astype(BF16)


def _attn_b(qz, kv, lq1, lk1, lq2, lk2, subln_g, *, tq, tk, lam_init):
    B, S, _ = qz.shape
    hw = 2 * HEAD_DIM
    rows = 2 * tq
    k_off = 0
    v_off = B_WIDTH // hw
    kern = functools.partial(_attn_b_kernel, lam_init=lam_init)
    vec = lambda a: a.reshape(1, -1)
    small = lambda n: pl.BlockSpec((1, n), lambda b, h, i: (0, 0))
    return pl.pallas_call(
        kern,
        grid=(B, B_HEADS, S // tq),
        in_specs=[
            pl.BlockSpec((1, tq, hw), lambda b, h, i: (b, i, h)),
            pl.BlockSpec((1, S, hw), lambda b, h, i: (b, 0, k_off + h)),
            pl.BlockSpec((1, S, hw), lambda b, h, i: (b, 0, v_off + h)),
            small(HEAD_DIM), small(HEAD_DIM), small(HEAD_DIM), small(HEAD_DIM), small(hw),
        ],
        out_specs=pl.BlockSpec((1, tq, hw), lambda b, h, i: (b, i, h)),
        out_shape=jax.ShapeDtypeStruct((B, S, B_WIDTH), BF16),
        scratch_shapes=[
            pltpu.VMEM((2, rows, tk), F32),
            pltpu.VMEM((2, rows, LANES), F32),
            pltpu.VMEM((rows, LANES), F32),
            pltpu.VMEM((rows, LANES), F32),
            pltpu.VMEM((rows, hw), F32),
        ],
        compiler_params=_compiler_params(3),
        name="attn_diff",
    )(qz, kv, kv, vec(lq1), vec(lk1), vec(lq2), vec(lk2), vec(subln_g))


def _outproj_kernel(o_ref, z_ref, w_ref, x_ref, *rest, emit_h):
    if emit_h:
        g_ref, out_ref, h_ref = rest
    else:
        (out_ref,) = rest
    z = z_ref[...].astype(F32)
    gated = (o_ref[...].astype(F32) * (z * jax.nn.sigmoid(z))).astype(BF16)
    out = x_ref[...] + jnp.dot(gated, w_ref[...], preferred_element_type=F32)
    out_ref[...] = out
    if emit_h:
        h_ref[...] = _rms(out, g_ref[...])


def _outproj(o, qz, w, x, g_next, *, tm):
    T, D = x.shape
    K = o.shape[1]
    emit_h = g_next is not None
    in_specs = [
        pl.BlockSpec((tm, K), lambda i: (i, 0)),
        pl.BlockSpec((tm, K), lambda i: (i, 1)),
        pl.BlockSpec((K, D), lambda i: (0, 0)),
        pl.BlockSpec((tm, D), lambda i: (i, 0)),
    ]
    args = [o, qz, w, x]
    out_specs = [pl.BlockSpec((tm, D), lambda i: (i, 0))]
    out_shape = [jax.ShapeDtypeStruct((T, D), F32)]
    if emit_h:
        in_specs.append(pl.BlockSpec((1, D), lambda i: (0, 0)))
        args.append(g_next.reshape(1, D))
        out_specs.append(pl.BlockSpec((tm, D), lambda i: (i, 0)))
        out_shape.append(jax.ShapeDtypeStruct((T, D), BF16))
    res = pl.pallas_call(
        functools.partial(_outproj_kernel, emit_h=emit_h),
        grid=(T // tm,),
        in_specs=in_specs,
        out_specs=out_specs,
        out_shape=out_shape,
        compiler_params=_compiler_params(1),
        name="outproj",
    )(*args)
    return (res[0], res[1]) if emit_h else (res[0], None)


def _rope_cos_sin(pos, dim):
    inv = 1.0 / (ROPE_THETA ** (jnp.arange(0, dim, 2, dtype=F32) / dim))
    ang = pos[:, None] * inv[None, :]
    return jnp.cos(ang), jnp.sin(ang)


Q_SCALE_LOG2 = HEAD_DIM ** -0.5 * math.log2(math.e)

_Q = HEAD_DIM // 4
AXIAL_PERM = tuple(list(range(0, _Q)) + list(range(2 * _Q, 3 * _Q))
                   + list(range(_Q, 2 * _Q)) + list(range(3 * _Q, 4 * _Q)))


def _tables(cos, sin, gain, scale):
    g = gain.astype(F32)
    t0 = jnp.concatenate([cos, cos], axis=1) * g[None, :]
    t1 = jnp.concatenate([-sin, sin], axis=1) * jnp.roll(g, ROPE_SHIFT)[None, :]
    return jnp.stack([t0, t1]) * scale


def _axial_tables(seq, q_gain, k_gain):
    n_rows = seq // GRID_W
    row = jnp.repeat(jnp.arange(n_rows, dtype=F32), GRID_W)
    col = jnp.tile(jnp.arange(GRID_W, dtype=F32), n_rows)
    half = HEAD_DIM // 2
    cr, sr = _rope_cos_sin(row, half)
    cc, sc = _rope_cos_sin(col, half)
    cos = jnp.concatenate([cr, cc], axis=1)
    sin = jnp.concatenate([sr, sc], axis=1)
    perm = jnp.array(AXIAL_PERM)
    return _tables(cos, sin, q_gain[perm], Q_SCALE_LOG2), _tables(cos, sin, k_gain[perm], 1.0)


def _rope1d_tables(seq, q_gain, k_gain):
    cos, sin = _rope_cos_sin(jnp.arange(seq, dtype=F32), HEAD_DIM)
    return _tables(cos, sin, q_gain, Q_SCALE_LOG2), _tables(cos, sin, k_gain, 1.0)


def _permute_heads(w):
    d, n = w.shape
    return w.reshape(d, n // HEAD_DIM, HEAD_DIM)[:, :, jnp.array(AXIAL_PERM)].reshape(d, n)


def _tile(n, pref):
    return pref if n % pref == 0 else n


def _project(h, w_qz, w_kv, tabs, q_width, k_width, S, tiles):
    qtab, ktab = tabs
    common = dict(seq=S, tm=_tile(S, tiles["proj_tm"]), tn=tiles["proj_tn"])
    qz = _proj(h, w_qz, qtab, rope_width=q_width, name="proj_qz", **common)
    kv = _proj(h, w_kv, ktab, rope_width=k_width, name="proj_kv", **common)
    return qz, kv


def _gqa_layer(xf, h, B, S, w_qz, w_kv, q_g, k_g, w_out, g_next, *, tiles):
    qz, kv = _project(h, w_qz, w_kv, _axial_tables(S, q_g, k_g), A_WIDTH, A_KV_WIDTH, S, tiles)
    o = _attn_a(qz.reshape(B, S, -1), kv.reshape(B, S, -1),
                tq=_tile(S, tiles["a_tq"]), tk=_tile(S, tiles["a_tk"]))
    return _outproj(o.reshape(B * S, A_WIDTH), qz, w_out, xf, g_next,
                    tm=_tile(B * S, tiles["out_tm"]))


def _diff_layer(xf, h, B, S, layer_idx, w_qz, w_kv, q_g, k_g, lq1, lk1, lq2, lk2, subln_g, w_out, g_next,
                *, tiles):
    lam_init = 0.8 - 0.6 * math.exp(-0.3 * layer_idx)
    qz, kv = _project(h, w_qz, w_kv, _rope1d_tables(S, q_g, k_g), B_WIDTH, B_WIDTH, S, tiles)
    o = _attn_b(qz.reshape(B, S, -1), kv.reshape(B, S, -1), lq1, lk1, lq2, lk2, subln_g,
                tq=_tile(S, tiles["b_tq"]), tk=_tile(S, tiles["b_tk"]), lam_init=lam_init)
    return _outproj(o.reshape(B * S, B_WIDTH), qz, w_out, xf, g_next,
                    tm=_tile(B * S, tiles["out_tm"]))


TILES = dict(proj_tm=1024, proj_tn=512, a_tq=256, a_tk=2048, b_tq=512, b_tk=1024, out_tm=512)


def _prepare_weights(a_w_in, a_w_out, b_w_in, b_w_out):
    a = a_w_in.astype(BF16)
    b = b_w_in.astype(BF16)
    cat = lambda x, y: jnp.concatenate([x, y], axis=1)
    k0, v0, z0 = A_WIDTH, A_WIDTH + A_KV_WIDTH, A_WIDTH + 2 * A_KV_WIDTH
    a_qz = [cat(_permute_heads(a[j, :, :k0]), a[j, :, z0:]) for j in range(a.shape[0])]
    a_kv = [cat(_permute_heads(a[j, :, k0:v0]), a[j, :, v0:z0]) for j in range(a.shape[0])]
    b_qz = [cat(b[j, :, :B_WIDTH], b[j, :, 3 * B_WIDTH:]) for j in range(b.shape[0])]
    b_kv = [b[j, :, B_WIDTH:3 * B_WIDTH] for j in range(b.shape[0])]
    return a_qz, a_kv, a_w_out.astype(BF16), b_qz, b_kv, b_w_out.astype(BF16)


def _trunk(x, weights, params, tiles):
    a_qz, a_kv, a_w_out, b_qz, b_kv, b_w_out = weights
    (a_norm, a_q_norm, a_k_norm, b_norm, b_q_norm, b_k_norm,
     b_lq1, b_lk1, b_lq2, b_lk2, b_subln) = params
    B, S, D = x.shape
    xf = x.reshape(B * S, D)
    norm_gain = lambda i: (a_norm if i % 2 == 0 else b_norm)[i // 2]
    h = _rmsnorm(xf, norm_gain(0), tm=_tile(B * S, tiles["out_tm"]))
    for i in range(DEPTH):
        j = i // 2
        g_next = norm_gain(i + 1) if i + 1 < DEPTH else None
        if i % 2 == 0:
            xf, h = _gqa_layer(xf, h, B, S, a_qz[j], a_kv[j], a_q_norm[j], a_k_norm[j], a_w_out[j],
                               g_next, tiles=tiles)
        else:
            xf, h = _diff_layer(xf, h, B, S, i, b_qz[j], b_kv[j], b_q_norm[j], b_k_norm[j],
                                b_lq1[j], b_lk1[j], b_lq2[j], b_lk2[j], b_subln[j], b_w_out[j],
                                g_next, tiles=tiles)
    return xf.reshape(B, S, D)


def kernel(x_prompt, x_sample, a_norm, a_w_in, a_q_norm, a_k_norm, a_w_out, b_norm, b_w_in, b_q_norm, b_k_norm, b_lambda_q1, b_lambda_k1, b_lambda_q2, b_lambda_k2, b_subln, b_w_out):
    weights = _prepare_weights(a_w_in, a_w_out, b_w_in, b_w_out)
    params = (a_norm, a_q_norm, a_k_norm, b_norm, b_q_norm, b_k_norm,
              b_lambda_q1, b_lambda_k1, b_lambda_q2, b_lambda_k2, b_subln)
    y_prompt = _trunk(x_prompt, weights, params, TILES)
    y_sample = _trunk(x_sample, weights, params, TILES)
    return (y_prompt, y_sample)
```

```python
import functools
import math

import jax
import jax.numpy as jnp
from jax import lax
from jax.experimental import pallas as pl
from jax.experimental.pallas import tpu as pltpu

D_MODEL = 2048
DEPTH = 4
GRID_W = 64
HEAD_DIM = 128
ROPE_THETA = 10000.0
EPS = 1e-6
A_HEADS = D_MODEL // HEAD_DIM
A_KV_HEADS = A_HEADS // 4
A_GROUP = A_HEADS // A_KV_HEADS
A_WIDTH = A_HEADS * HEAD_DIM
A_KV_WIDTH = A_KV_HEADS * HEAD_DIM
B_HEADS = D_MODEL // (2 * HEAD_DIM)
B_WIDTH = B_HEADS * 2 * HEAD_DIM

LANES = 128
VMEM_LIMIT_BYTES = 56 * 1024 * 1024
ROPE_SHIFT = HEAD_DIM // 2

F32 = jnp.float32
BF16 = jnp.bfloat16


def _compiler_params(n_axes):
    return pltpu.CompilerParams(
        dimension_semantics=("arbitrary",) * n_axes,
        vmem_limit_bytes=VMEM_LIMIT_BYTES,
    )


SOFTMAX_ROW_BLOCK = 256


def _rms(x, g):
    ms = jnp.mean(x * x, axis=-1, keepdims=True)
    return (x * lax.rsqrt(ms + EPS) * g).astype(BF16)


def _rmsnorm_kernel(x_ref, g_ref, h_ref):
    h_ref[...] = _rms(x_ref[...], g_ref[...])


def _rmsnorm(x, g, *, tm):
    T, D = x.shape
    return pl.pallas_call(
        _rmsnorm_kernel,
        grid=(T // tm,),
        in_specs=[pl.BlockSpec((tm, D), lambda i: (i, 0)), pl.BlockSpec((1, D), lambda i: (0, 0))],
        out_specs=pl.BlockSpec((tm, D), lambda i: (i, 0)),
        out_shape=jax.ShapeDtypeStruct((T, D), BF16),
        compiler_params=_compiler_params(1),
        name="rmsnorm",
    )(x, g.reshape(1, D))


def _proj_kernel(h_ref, w_ref, tab_ref, out_ref, *, tn, n_rope_blocks):
    h = h_ref[...]
    for j in range(out_ref.shape[1] // tn):
        cols = pl.ds(j * tn, tn)
        y = jnp.dot(h, w_ref[:, cols], preferred_element_type=F32)
        if j < n_rope_blocks:
            t0 = tab_ref[0]
            t1 = tab_ref[1]
            outs = []
            for hh in range(tn // HEAD_DIM):
                yh = y[:, hh * HEAD_DIM:(hh + 1) * HEAD_DIM]
                inv = lax.rsqrt(jnp.mean(yh * yh, axis=-1, keepdims=True) + EPS)
                outs.append((yh * t0 + pltpu.roll(yh, ROPE_SHIFT, 1) * t1) * inv)
            y = jnp.concatenate(outs, axis=1)
        out_ref[:, cols] = y.astype(BF16)


def _proj(h, w, tabs, *, rope_width, seq, tm, tn, name):
    T, D = h.shape
    N = w.shape[1]
    pos_blocks = seq // tm
    return pl.pallas_call(
        functools.partial(_proj_kernel, tn=tn, n_rope_blocks=rope_width // tn),
        grid=(T // tm,),
        in_specs=[
            pl.BlockSpec((tm, D), lambda i: (i, 0)),
            pl.BlockSpec((D, N), lambda i: (0, 0), pipeline_mode=pl.Buffered(1)),
            pl.BlockSpec((2, tm, LANES), lambda i: (0, i % pos_blocks, 0)),
        ],
        out_specs=pl.BlockSpec((tm, N), lambda i: (i, 0)),
        out_shape=jax.ShapeDtypeStruct((T, N), BF16),
        compiler_params=_compiler_params(1),
        name=name,
    )(h, w, tabs)


def _score_stage(s, slot, s_ref, mx_ref):
    s_ref[slot] = s
    mx_ref[slot] = jnp.broadcast_to(jnp.max(s, axis=1, keepdims=True), mx_ref.shape[1:])


def _softmax_stage(slot, v, s_ref, mx_ref, m_ref, l_ref, acc_ref):
    rows, tk = s_ref.shape[1:]
    dv = acc_ref.shape[1]
    for r0 in range(0, rows, SOFTMAX_ROW_BLOCK):
        rb = pl.ds(r0, min(SOFTMAX_ROW_BLOCK, rows - r0))
        m_prev = m_ref[rb, :]
        m_next = jnp.maximum(m_prev, mx_ref[slot, rb, :])
        alpha = jnp.exp2(m_prev - m_next)
        p = jnp.exp2(s_ref[slot, rb, :] - jnp.tile(m_next, (1, tk // LANES)))
        if l_ref is not None:
            part = p[:, :LANES]
            for t in range(1, tk // LANES):
                part = part + p[:, t * LANES:(t + 1) * LANES]
            l_ref[rb, :] = alpha * l_ref[rb, :] + part
        pv = jnp.dot(p.astype(BF16), v, preferred_element_type=F32)
        acc_ref[rb, :] = acc_ref[rb, :] * jnp.tile(alpha, (1, dv // LANES)) + pv
        m_ref[rb, :] = m_next


def _pipelined_kv_loop(n_chunks, score_stage, softmax_stage):
    score_stage(0, 0)
    for c in range(n_chunks):
        if c + 1 < n_chunks:
            score_stage(c + 1, (c + 1) % 2)
        softmax_stage(c, c % 2)


def _gate(o, z_ref):
    z = z_ref[0].astype(F32)
    return (o * (z * jax.nn.sigmoid(z))).astype(BF16)


def _attn_a_kernel(q_ref, k_ref, v_ref, z_ref, o_ref, s_ref, mx_ref, m_ref, acc_ref, *, group):
    tq = q_ref.shape[1]
    seq = k_ref.shape[1]
    tk = s_ref.shape[2]
    q = q_ref[0]
    qs = jnp.concatenate([q[:, g * HEAD_DIM:(g + 1) * HEAD_DIM] for g in range(group)], axis=0)
    m_ref[...] = jnp.full(m_ref.shape, -jnp.inf, F32)
    acc_ref[...] = jnp.zeros(acc_ref.shape, F32)
    ones = jnp.ones((tk, LANES), BF16)

    def score_stage(c, slot):
        k = k_ref[0, pl.ds(pl.multiple_of(c * tk, tk), tk), :]
        s = lax.dot_general(qs, k, (((1,), (1,)), ((), ())), preferred_element_type=F32)
        _score_stage(s, slot, s_ref, mx_ref)

    def softmax_stage(c, slot):
        v = v_ref[0, pl.ds(pl.multiple_of(c * tk, tk), tk), :]
        v_ext = jnp.concatenate([v, ones], axis=1)
        _softmax_stage(slot, v_ext, s_ref, mx_ref, m_ref, None, acc_ref)

    _pipelined_kv_loop(seq // tk, score_stage, softmax_stage)
    acc = acc_ref[...]
    out = acc[:, :HEAD_DIM] / acc[:, HEAD_DIM:]
    o_ref[0] = _gate(jnp.concatenate([out[g * tq:(g + 1) * tq] for g in range(group)], axis=1), z_ref)


def _attn_a(qz, kv, *, tq, tk):
    B, S, _ = qz.shape
    gw = A_GROUP * HEAD_DIM
    rows = A_GROUP * tq
    k_off = 0
    v_off = A_KV_WIDTH // HEAD_DIM
    kern = functools.partial(_attn_a_kernel, group=A_GROUP)
    return pl.pallas_call(
        kern,
        grid=(B, A_KV_HEADS, S // tq),
        in_specs=[
            pl.BlockSpec((1, tq, gw), lambda b, h, i: (b, i, h)),
            pl.BlockSpec((1, S, HEAD_DIM), lambda b, h, i: (b, 0, k_off + h)),
            pl.BlockSpec((1, S, HEAD_DIM), lambda b, h, i: (b, 0, v_off + h)),
            pl.BlockSpec((1, tq, gw), lambda b, h, i: (b, i, A_KV_HEADS + h)),
        ],
        out_specs=pl.BlockSpec((1, tq, gw), lambda b, h, i: (b, i, h)),
        out_shape=jax.ShapeDtypeStruct((B, S, A_WIDTH), BF16),
        scratch_shapes=[
            pltpu.VMEM((2, rows, tk), F32),
            pltpu.VMEM((2, rows, LANES), F32),
            pltpu.VMEM((rows, LANES), F32),
            pltpu.VMEM((rows, 2 * HEAD_DIM), F32),
        ],
        compiler_params=_compiler_params(3),
        name="attn_gqa",
    )(qz, kv, kv, qz)


def _attn_b_kernel(q_ref, k_ref, v_ref, z_ref, lq1_ref, lk1_ref, lq2_ref, lk2_ref, sg_ref,
                   o_ref, s_ref, mx_ref, m_ref, l_ref, acc_ref, *, lam_init):
    tq = q_ref.shape[1]
    seq = k_ref.shape[1]
    tk = s_ref.shape[2]
    q = q_ref[0]
    q1 = q[:, :HEAD_DIM]
    q2 = q[:, HEAD_DIM:]
    m_ref[...] = jnp.full(m_ref.shape, -jnp.inf, F32)
    l_ref[...] = jnp.zeros(l_ref.shape, F32)
    acc_ref[...] = jnp.zeros(acc_ref.shape, F32)
    dn = (((1,), (1,)), ((), ()))

    def score_stage(c, slot):
        k = k_ref[0, pl.ds(pl.multiple_of(c * tk, tk), tk), :]
        s1 = lax.dot_general(q1, k[:, :HEAD_DIM], dn, preferred_element_type=F32)
        s2 = lax.dot_general(q2, k[:, HEAD_DIM:], dn, preferred_element_type=F32)
        _score_stage(jnp.concatenate([s1, s2], axis=0), slot, s_ref, mx_ref)

    def softmax_stage(c, slot):
        v = v_ref[0, pl.ds(pl.multiple_of(c * tk, tk), tk), :]
        _softmax_stage(slot, v, s_ref, mx_ref, m_ref, l_ref, acc_ref)

    _pipelined_kv_loop(seq // tk, score_stage, softmax_stage)

    lam = (jnp.exp(jnp.sum(lq1_ref[...] * lk1_ref[...], axis=-1, keepdims=True))
           - jnp.exp(jnp.sum(lq2_ref[...] * lk2_ref[...], axis=-1, keepdims=True)) + lam_init)
    l = jnp.sum(l_ref[...], axis=1, keepdims=True)
    out = acc_ref[...] / l
    o = out[:tq] - lam * out[tq:]
    ms = jnp.mean(o * o, axis=-1, keepdims=True)
    o = o * lax.rsqrt(ms + EPS) * sg_ref[...] * (1.0 - lam_init)
    o_ref[0] = _gate(o, z_ref)


def _attn_b(qz, kv, lq1, lk1, lq2, lk2, subln_g, *, tq, tk, lam_init):
    B, S, _ = qz.shape
    hw = 2 * HEAD_DIM
    rows = 2 * tq
    k_off = 0
    v_off = B_WIDTH // hw
    kern = functools.partial(_attn_b_kernel, lam_init=lam_init)
    vec = lambda a: a.reshape(1, -1)
    small = lambda n: pl.BlockSpec((1, n), lambda b, h, i: (0, 0))
    return pl.pallas_call(
        kern,
        grid=(B, B_HEADS, S // tq),
        in_specs=[
            pl.BlockSpec((1, tq, hw), lambda b, h, i: (b, i, h)),
            pl.BlockSpec((1, S, hw), lambda b, h, i: (b, 0, k_off + h)),
            pl.BlockSpec((1, S, hw), lambda b, h, i: (b, 0, v_off + h)),
            pl.BlockSpec((1, tq, hw), lambda b, h, i: (b, i, B_HEADS + h)),
            small(HEAD_DIM), small(HEAD_DIM), small(HEAD_DIM), small(HEAD_DIM), small(hw),
        ],
        out_specs=pl.BlockSpec((1, tq, hw), lambda b, h, i: (b, i, h)),
        out_shape=jax.ShapeDtypeStruct((B, S, B_WIDTH), BF16),
        scratch_shapes=[
            pltpu.VMEM((2, rows, tk), F32),
            pltpu.VMEM((2, rows, LANES), F32),
            pltpu.VMEM((rows, LANES), F32),
            pltpu.VMEM((rows, LANES), F32),
            pltpu.VMEM((rows, hw), F32),
        ],
        compiler_params=_compiler_params(3),
        name="attn_diff",
    )(qz, kv, kv, qz, vec(lq1), vec(lk1), vec(lq2), vec(lk2), vec(subln_g))


def _outproj_kernel(o_ref, w_ref, x_ref, *rest, emit_h):
    if emit_h:
        g_ref, out_ref, h_ref = rest
    else:
        (out_ref,) = rest
    out = x_ref[...] + jnp.dot(o_ref[...], w_ref[...], preferred_element_type=F32)
    out_ref[...] = out
    if emit_h:
        h_ref[...] = _rms(out, g_ref[...])


def _outproj(o, w, x, g_next, *, tm):
    T, D = x.shape
    K = o.shape[1]
    emit_h = g_next is not None
    in_specs = [
        pl.BlockSpec((tm, K), lambda i: (i, 0)),
        pl.BlockSpec((K, D), lambda i: (0, 0)),
        pl.BlockSpec((tm, D), lambda i: (i, 0)),
    ]
    args = [o, w, x]
    out_specs = [pl.BlockSpec((tm, D), lambda i: (i, 0))]
    out_shape = [jax.ShapeDtypeStruct((T, D), F32)]
    if emit_h:
        in_specs.append(pl.BlockSpec((1, D), lambda i: (0, 0)))
        args.append(g_next.reshape(1, D))
        out_specs.append(pl.BlockSpec((tm, D), lambda i: (i, 0)))
        out_shape.append(jax.ShapeDtypeStruct((T, D), BF16))
    res = pl.pallas_call(
        functools.partial(_outproj_kernel, emit_h=emit_h),
        grid=(T // tm,),
        in_specs=in_specs,
        out_specs=out_specs,
        out_shape=out_shape,
        compiler_params=_compiler_params(1),
        name="outproj",
    )(*args)
    return (res[0], res[1]) if emit_h else (res[0], None)


def _rope_cos_sin(pos, dim):
    inv = 1.0 / (ROPE_THETA ** (jnp.arange(0, dim, 2, dtype=F32) / dim))
    ang = pos[:, None] * inv[None, :]
    return jnp.cos(ang), jnp.sin(ang)


Q_SCALE_LOG2 = HEAD_DIM ** -0.5 * math.log2(math.e)

_Q = HEAD_DIM // 4
AXIAL_PERM = tuple(list(range(0, _Q)) + list(range(2 * _Q, 3 * _Q))
                   + list(range(_Q, 2 * _Q)) + list(range(3 * _Q, 4 * _Q)))


def _tables(cos, sin, gain, scale):
    g = gain.astype(F32)
    t0 = jnp.concatenate([cos, cos], axis=1) * g[None, :]
    t1 = jnp.concatenate([-sin, sin], axis=1) * jnp.roll(g, ROPE_SHIFT)[None, :]
    return jnp.stack([t0, t1]) * scale


def _axial_tables(seq, q_gain, k_gain):
    n_rows = seq // GRID_W
    row = jnp.repeat(jnp.arange(n_rows, dtype=F32), GRID_W)
    col = jnp.tile(jnp.arange(GRID_W, dtype=F32), n_rows)
    half = HEAD_DIM // 2
    cr, sr = _rope_cos_sin(row, half)
    cc, sc = _rope_cos_sin(col, half)
    cos = jnp.concatenate([cr, cc], axis=1)
    sin = jnp.concatenate([sr, sc], axis=1)
    perm = jnp.array(AXIAL_PERM)
    return _tables(cos, sin, q_gain[perm], Q_SCALE_LOG2), _tables(cos, sin, k_gain[perm], 1.0)


def _rope1d_tables(seq, q_gain, k_gain):
    cos, sin = _rope_cos_sin(jnp.arange(seq, dtype=F32), HEAD_DIM)
    return _tables(cos, sin, q_gain, Q_SCALE_LOG2), _tables(cos, sin, k_gain, 1.0)


def _permute_heads(w):
    d, n = w.shape
    return w.reshape(d, n // HEAD_DIM, HEAD_DIM)[:, :, jnp.array(AXIAL_PERM)].reshape(d, n)


def _tile(n, pref):
    return pref if n % pref == 0 else n


def _project(h, w_qz, w_kv, tabs, q_width, k_width, S, tiles):
    qtab, ktab = tabs
    common = dict(seq=S, tm=_tile(S, tiles["proj_tm"]), tn=tiles["proj_tn"])
    qz = _proj(h, w_qz, qtab, rope_width=q_width, name="proj_qz", **common)
    kv = _proj(h, w_kv, ktab, rope_width=k_width, name="proj_kv", **common)
    return qz, kv


def _gqa_layer(xf, h, B, S, w_qz, w_kv, q_g, k_g, w_out, g_next, *, tiles):
    qz, kv = _project(h, w_qz, w_kv, _axial_tables(S, q_g, k_g), A_WIDTH, A_KV_WIDTH, S, tiles)
    o = _attn_a(qz.reshape(B, S, -1), kv.reshape(B, S, -1),
                tq=_tile(S, tiles["a_tq"]), tk=_tile(S, tiles["a_tk"]))
    return _outproj(o.reshape(B * S, A_WIDTH), w_out, xf, g_next,
                    tm=_tile(B * S, tiles["out_tm"]))


def _diff_layer(xf, h, B, S, layer_idx, w_qz, w_kv, q_g, k_g, lq1, lk1, lq2, lk2, subln_g, w_out, g_next,
                *, tiles):
    lam_init = 0.8 - 0.6 * math.exp(-0.3 * layer_idx)
    qz, kv = _project(h, w_qz, w_kv, _rope1d_tables(S, q_g, k_g), B_WIDTH, B_WIDTH, S, tiles)
    o = _attn_b(qz.reshape(B, S, -1), kv.reshape(B, S, -1), lq1, lk1, lq2, lk2, subln_g,
                tq=_tile(S, tiles["b_tq"]), tk=_tile(S, tiles["b_tk"]), lam_init=lam_init)
    return _outproj(o.reshape(B * S, B_WIDTH), w_out, xf, g_next,
                    tm=_tile(B * S, tiles["out_tm"]))


TILES = dict(proj_tm=1024, proj_tn=512, a_tq=256, a_tk=2048, b_tq=512, b_tk=1024, out_tm=512)


def _prepare_weights(a_w_in, a_w_out, b_w_in, b_w_out):
    a = a_w_in.astype(BF16)
    b = b_w_in.astype(BF16)
    cat = lambda x, y: jnp.concatenate([x, y], axis=1)
    k0, v0, z0 = A_WIDTH, A_WIDTH + A_KV_WIDTH, A_WIDTH + 2 * A_KV_WIDTH
    a_qz = [cat(_permute_heads(a[j, :, :k0]), a[j, :, z0:]) for j in range(a.shape[0])]
    a_kv = [cat(_permute_heads(a[j, :, k0:v0]), a[j, :, v0:z0]) for j in range(a.shape[0])]
    b_qz = [cat(b[j, :, :B_WIDTH], b[j, :, 3 * B_WIDTH:]) for j in range(b.shape[0])]
    b_kv = [b[j, :, B_WIDTH:3 * B_WIDTH] for j in range(b.shape[0])]
    return a_qz, a_kv, a_w_out.astype(BF16), b_qz, b_kv, b_w_out.astype(BF16)


def _trunk(x, weights, params, tiles):
    a_qz, a_kv, a_w_out, b_qz, b_kv, b_w_out = weights
    (a_norm, a_q_norm, a_k_norm, b_norm, b_q_norm, b_k_norm,
     b_lq1, b_lk1, b_lq2, b_lk2, b_subln) = params
    B, S, D = x.shape
    xf = x.reshape(B * S, D)
    norm_gain = lambda i: (a_norm if i % 2 == 0 else b_norm)[i // 2]
    h = _rmsnorm(xf, norm_gain(0), tm=_tile(B * S, tiles["out_tm"]))
    for i in range(DEPTH):
        j = i // 2
        g_next = norm_gain(i + 1) if i + 1 < DEPTH else None
        if i % 2 == 0:
            xf, h = _gqa_layer(xf, h, B, S, a_qz[j], a_kv[j], a_q_norm[j], a_k_norm[j], a_w_out[j],
                               g_next, tiles=tiles)
        else:
            xf, h = _diff_layer(xf, h, B, S, i, b_qz[j], b_kv[j], b_q_norm[j], b_k_norm[j],
                                b_lq1[j], b_lk1[j], b_lq2[j], b_lk2[j], b_subln[j], b_w_out[j],
                                g_next, tiles=tiles)
    return xf.reshape(B, S, D)


def kernel(x_prompt, x_sample, a_norm, a_w_in, a_q_norm, a_k_norm, a_w_out, b_norm, b_w_in, b_q_norm, b_k_norm, b_lambda_q1, b_lambda_k1, b_lambda_q2, b_lambda_k2, b_subln, b_w_out):
    weights = _prepare_weights(a_w_in, a_w_out, b_w_in, b_w_out)
    params = (a_norm, a_q_norm, a_k_norm, b_norm, b_q_norm, b_k_norm,
              b_lambda_q1, b_lambda_k1, b_lambda_q2, b_lambda_k2, b_subln)
    y_prompt = _trunk(x_prompt, weights, params, TILES)
    y_sample = _trunk(x_sample, weights, params, TILES)
    return (y_prompt, y_sample)
```
